```python
import math
import jax, jax.numpy as jnp
from jax import lax
import numpy as np

D_MODEL = 1024
BATCH = 8
SEQ = 4096
DEPTH = 4
DEC_BATCH = 8
DEC_SEQ = 64
PAST_LEN = 4096

F32 = jnp.float32
CHUNK = 64
Q_BLOCK = 128
EPS = 1e-6
N_EVEN = (DEPTH + 1) // 2
N_ODD = DEPTH // 2
H_A = 4
DK_A = 128
DV_A = 128
CONV_W = 4
H_B = 8
Q_LORA = 384
KV_LORA = 256
D_NOPE = 64
D_ROPE = 32
DV_B = 64
ROPE_THETA = 10000.0
H_C = 8
DK_C = 128
DV_C = 128

W_A = H_A * DV_A
W_B = H_B * DV_B
W_C = H_C * DV_C
QKV_A = H_A * (2 * DK_A + DV_A)
IN_E_SPLITS = (QKV_A, H_A, H_A, W_A, Q_LORA, KV_LORA + D_ROPE, W_B)
IN_E = QKV_A + 2 * H_A + W_A + Q_LORA + KV_LORA + D_ROPE + W_B
IN_O = 2 * H_C * DK_C + 2 * W_C

kernel_name = 'hybrid_streaming_gdn_mla_hgrn2_step'


def split_last(x, sizes):
    return jnp.split(x, np.cumsum(sizes)[:-1].tolist(), axis=-1)


def rmsnorm(x, g):
    xf = x.astype(F32)
    y = xf * lax.rsqrt(jnp.mean(xf * xf, axis=-1, keepdims=True) + EPS)
    return (y * g.astype(F32)).astype(x.dtype)


def l2norm(x):
    xf = x.astype(F32)
    return xf * lax.rsqrt(jnp.sum(xf * xf, axis=-1, keepdims=True) + EPS)


def rope(x, pos):
    half = D_ROPE // 2
    inv = ROPE_THETA ** (-jnp.arange(half, dtype=F32) / half)
    ang = pos.astype(F32)[:, None] * inv[None, :]
    cos, sin = jnp.cos(ang)[:, None, :], jnp.sin(ang)[:, None, :]
    xf = x.astype(F32)
    x1, x2 = xf[..., :half], xf[..., half:]
    return jnp.concatenate([x1 * cos - x2 * sin, x1 * sin + x2 * cos], axis=-1).astype(x.dtype)


def causal_dwconv(x, buf, w):
    xp = jnp.concatenate([buf.astype(x.dtype), x], axis=1)
    y = lax.conv_general_dilated(xp, w[:, None, :].astype(x.dtype), window_strides=(1,), padding='VALID',
                                 dimension_numbers=('NWC', 'WIO', 'NWC'), feature_group_count=x.shape[-1])
    return y, xp[:, -(CONV_W - 1):]


def scan_chunks(chunk_fn, S0, seqs):
    T = seqs[0].shape[2]
    L = CHUNK if T % CHUNK == 0 else T
    n = T // L

    def split(a):
        return jnp.moveaxis(a.reshape(a.shape[:2] + (n, L) + a.shape[3:]), 2, 0)

    S, o = lax.scan(lambda S, xs: chunk_fn(S, *xs), S0, tuple(split(a) for a in seqs))
    o = jnp.moveaxis(o, 0, 2)
    return o.reshape(o.shape[:2] + (T,) + o.shape[4:]), S


def gdn_chunk(S, q, k, v, beta, g):
    L = q.shape[2]
    tril = jnp.tril(jnp.ones((L, L), bool))
    strict = jnp.tril(jnp.ones((L, L), bool), -1)
    gc = jnp.cumsum(g, axis=-1)
    decay = jnp.exp(jnp.where(tril, gc[..., :, None] - gc[..., None, :], -jnp.inf))
    kb = k * beta[..., None]
    M = jnp.einsum('bhid,bhjd->bhij', kb, k) * jnp.where(strict, decay, 0.0)
    rhs = jnp.concatenate([v * beta[..., None], kb * jnp.exp(gc)[..., None]], axis=-1)
    uw = lax.linalg.triangular_solve(M + jnp.eye(L, dtype=F32), rhs, left_side=True, lower=True, unit_diagonal=True)
    u, w = uw[..., :DV_A], uw[..., DV_A:]
    v_new = u - jnp.einsum('bhld,bhde->bhle', w, S)
    attn = jnp.einsum('bhid,bhjd->bhij', q, k) * decay
    o = jnp.einsum('bhld,bhde->bhle', q * jnp.exp(gc)[..., None], S) + jnp.einsum('bhij,bhje->bhie', attn, v_new)
    gl = gc[..., -1:]
    S_new = S * jnp.exp(gl)[..., None] + jnp.einsum('bhld,bhle->bhde', k * jnp.exp(gl - gc)[..., None], v_new)
    return S_new, o


def hgrn_chunk(S, q, k, i, logf):
    L = q.shape[2]
    G = jnp.cumsum(logf, axis=2)
    causal = jnp.tril(jnp.ones((L, L), bool))
    rel = jnp.exp(jnp.where(causal[None, None, :, :, None], G[:, :, :, None, :] - G[:, :, None, :, :], -jnp.inf))
    scores = jnp.einsum('bhtd,bhsd,bhtsd->bhts', q, k, rel)
    o = jnp.einsum('bhtd,bhde->bhte', q * jnp.exp(G), S) + jnp.einsum('bhts,bhse->bhte', scores, i)
    GL = G[:, :, -1:, :]
    S_new = S * jnp.exp(GL[:, :, 0, :])[..., None] + jnp.einsum('bhsd,bhse->bhde', k * jnp.exp(GL - G), i)
    return S_new, o


def chunk_causal_attention(q, k, v, q_pos, k_pos):
    B, Tq, H, Dq = q.shape
    Dv = v.shape[-1]
    scale = (D_NOPE + D_ROPE) ** -0.5
    k_chunk = k_pos // CHUNK

    def block(args):
        qb, pb = args
        s = jnp.einsum('bqhd,bkhd->bhqk', qb, k, preferred_element_type=F32) * scale
        mask = k_chunk[None, :] <= (pb // CHUNK)[:, None]
        p = jax.nn.softmax(jnp.where(mask[None, None], s, -jnp.inf), axis=-1)
        return jnp.einsum('bhqk,bkhd->bqhd', p.astype(v.dtype), v)

    if Tq > Q_BLOCK and Tq % Q_BLOCK == 0:
        nb = Tq // Q_BLOCK
        qb = q.reshape(B, nb, Q_BLOCK, H, Dq).transpose(1, 0, 2, 3, 4)
        o = lax.map(block, (qb, q_pos.reshape(nb, Q_BLOCK)))
        return o.transpose(1, 0, 2, 3, 4).reshape(B, Tq, H, Dv)
    return block((q, q_pos))


def even_mixer(h, pos, k_pos, conv_buf, s_delta, ckv_past, kr_past, w_in, conv_w, a_log, dt_bias,
               g_norm_a, g_q, w_uq, g_kv, w_ukv, w_out):
    B, T, _ = h.shape
    qkv, b_logit, a_logit, z_a, cq, ckv_kr, z_b = split_last(h @ w_in, IN_E_SPLITS)
    qkv_c, conv_new = causal_dwconv(qkv, conv_buf, conv_w)
    qa, ka, va = split_last(jax.nn.silu(qkv_c), (H_A * DK_A, H_A * DK_A, H_A * DV_A))
    qa = (l2norm(qa.reshape(B, T, H_A, DK_A)) * DK_A ** -0.5).transpose(0, 2, 1, 3)
    ka = l2norm(ka.reshape(B, T, H_A, DK_A)).transpose(0, 2, 1, 3)
    va = va.reshape(B, T, H_A, DV_A).astype(F32).transpose(0, 2, 1, 3)
    beta = jax.nn.sigmoid(b_logit.astype(F32)).transpose(0, 2, 1)
    g = (-jnp.exp(a_log.astype(F32)) * jax.nn.softplus(a_logit.astype(F32) + dt_bias.astype(F32))).transpose(0, 2, 1)
    o_a, S_new = scan_chunks(gdn_chunk, s_delta.astype(F32), (qa, ka, va, beta, g))
    o_a = rmsnorm(o_a.transpose(0, 2, 1, 3), g_norm_a).astype(h.dtype)
    o_a = (o_a * jax.nn.silu(z_a).reshape(B, T, H_A, DV_A)).reshape(B, T, W_A)
    q_b = (rmsnorm(cq, g_q) @ w_uq).reshape(B, T, H_B, D_NOPE + D_ROPE)
    q_b = jnp.concatenate([q_b[..., :D_NOPE], rope(q_b[..., D_NOPE:], pos)], axis=-1)
    c_kv = rmsnorm(ckv_kr[..., :KV_LORA], g_kv)
    k_pe = rope(ckv_kr[..., KV_LORA:][:, :, None, :], pos)[:, :, 0, :]
    ckv_all = jnp.concatenate([ckv_past.astype(h.dtype), c_kv], axis=1)
    kr_all = jnp.concatenate([kr_past.astype(h.dtype), k_pe], axis=1)
    Tk = ckv_all.shape[1]
    kv = (ckv_all @ w_ukv).reshape(B, Tk, H_B, D_NOPE + DV_B)
    k_b = jnp.concatenate([kv[..., :D_NOPE], jnp.broadcast_to(kr_all[:, :, None, :], (B, Tk, H_B, D_ROPE))], axis=-1)
    o_b = chunk_causal_attention(q_b, k_b, kv[..., D_NOPE:], pos, k_pos)
    o_b = o_b.reshape(B, T, W_B) * jax.nn.silu(z_b)
    out = jnp.concatenate([o_a, o_b], axis=-1) @ w_out
    return (out, S_new.astype(s_delta.dtype), conv_new.astype(conv_buf.dtype),
            c_kv.astype(ckv_past.dtype), k_pe.astype(kr_past.dtype))


def odd_mixer(h, s_hgrn, lb, w_in, g_norm_c, w_out):
    B, T, _ = h.shape
    q, f, i, z = split_last(h @ w_in, (H_C * DK_C, H_C * DK_C, W_C, W_C))
    fg = lb + (1.0 - lb) * jax.nn.sigmoid(f.astype(F32))

    def heads(a, d):
        return a.reshape(B, T, H_C, d).transpose(0, 2, 1, 3)

    qh = heads(jax.nn.silu(q.astype(F32)), DK_C)
    kh = heads(1.0 - fg, DK_C)
    lh = heads(jnp.log(fg), DK_C)
    ih = heads(i.astype(F32), DV_C)
    o, S_new = scan_chunks(hgrn_chunk, s_hgrn.astype(F32), (qh, kh, ih, lh))
    o = rmsnorm(o.transpose(0, 2, 1, 3), g_norm_c).astype(h.dtype)
    o = (o * jax.nn.silu(z).reshape(B, T, H_C, DV_C)).reshape(B, T, W_C)
    return o @ w_out, S_new.astype(s_hgrn.dtype)


def trunk(x, c, ckv_past, kr_past, s_delta, s_conv, s_hgrn, w_ada, b_ada, g_pre, g_post, w_in_e, conv_w,
          a_log, dt_bias, g_norm_a, g_q, w_uq, g_kv, w_ukv, w_out_e, w_in_o, lb_logits, g_norm_c, w_out_o):
    T = x.shape[1]
    past = ckv_past.shape[2]
    pos = past + jnp.arange(T, dtype=jnp.int32)
    k_pos = jnp.arange(past + T, dtype=jnp.int32)
    mod = jnp.einsum('bd,lde->lbe', jax.nn.silu(c), w_ada) + b_ada[:, None, :]
    p_lb = jax.nn.softmax(lb_logits.astype(F32), axis=0)
    lbs = jnp.cumsum(p_lb, axis=0) - p_lb[0]
    nd, nc, nk, nr, nh = [], [], [], [], []
    for l in range(DEPTH):
        shift, scale, gate = jnp.split(mod[l], 3, axis=-1)
        h = rmsnorm(x, g_pre[l]) * (1.0 + scale[:, None, :]) + shift[:, None, :]
        if l % 2 == 0:
            e = l // 2
            o, S, cb, ck, kr = even_mixer(h, pos, k_pos, s_conv[e], s_delta[e], ckv_past[e], kr_past[e],
                                          w_in_e[e], conv_w[e], a_log[e], dt_bias[e], g_norm_a[e],
                                          g_q[e], w_uq[e], g_kv[e], w_ukv[e], w_out_e[e])
            nd.append(S)
            nc.append(cb)
            nk.append(ck)
            nr.append(kr)
        else:
            j = l // 2
            o, S = odd_mixer(h, s_hgrn[j], lbs[l], w_in_o[j], g_norm_c[j], w_out_o[j])
            nh.append(S)
        x = x + gate[:, None, :] * rmsnorm(o, g_post[l])
    return (x, jnp.stack(nd), jnp.stack(nc), jnp.stack(nk), jnp.stack(nr), jnp.stack(nh))


def setup_inputs(seed: int = 0) -> dict:
    key = jax.random.key(seed)
    ks = jax.random.split(key, 32)
    D = D_MODEL

    def nrm(k, shape, s):
        return jax.random.normal(k, shape, F32) * s

    dt0 = jnp.exp(jax.random.uniform(ks[16], (N_EVEN, H_A), F32, math.log(1e-3), math.log(1e-1)))
    return {
        'x_prompt': nrm(ks[0], (BATCH, SEQ, D), 1.0),
        'x_sample': nrm(ks[1], (DEC_BATCH, DEC_SEQ, D), 1.0),
        'c_prompt': nrm(ks[2], (BATCH, D), 1.0),
        'c_sample': nrm(ks[3], (DEC_BATCH, D), 1.0),
        'cache_ckv': nrm(ks[4], (N_EVEN, DEC_BATCH, PAST_LEN, KV_LORA), 1.0),
        'cache_kr': nrm(ks[5], (N_EVEN, DEC_BATCH, PAST_LEN, D_ROPE), 1.0),
        'state_delta': nrm(ks[6], (N_EVEN, DEC_BATCH, H_A, DK_A, DV_A), 0.1),
        'state_conv': nrm(ks[7], (N_EVEN, DEC_BATCH, CONV_W - 1, QKV_A), 1.0),
        'state_hgrn': nrm(ks[8], (N_ODD, DEC_BATCH, H_C, DK_C, DV_C), 0.1),
        'w_ada': nrm(ks[9], (DEPTH, D, 3 * D), 0.5 * D ** -0.5),
        'b_ada': nrm(ks[10], (DEPTH, 3 * D), 0.01),
        'g_pre': 1.0 + nrm(ks[11], (DEPTH, D), 0.01),
        'g_post': 1.0 + nrm(ks[12], (DEPTH, D), 0.01),
        'w_in_e': nrm(ks[13], (N_EVEN, D, IN_E), D ** -0.5),
        'conv_w': nrm(ks[14], (N_EVEN, CONV_W, QKV_A), CONV_W ** -0.5),
        'a_log': jnp.log(jax.random.uniform(ks[15], (N_EVEN, H_A), F32, 1.0, 16.0)),
        'dt_bias': dt0 + jnp.log(-jnp.expm1(-dt0)),
        'g_norm_a': 1.0 + nrm(ks[17], (N_EVEN, DV_A), 0.01),
        'g_q': 1.0 + nrm(ks[18], (N_EVEN, Q_LORA), 0.01),
        'w_uq': nrm(ks[19], (N_EVEN, Q_LORA, H_B * (D_NOPE + D_ROPE)), Q_LORA ** -0.5),
        'g_kv': 1.0 + nrm(ks[20], (N_EVEN, KV_LORA), 0.01),
        'w_ukv': nrm(ks[21], (N_EVEN, KV_LORA, H_B * (D_NOPE + DV_B)), KV_LORA ** -0.5),
        'w_out_e': nrm(ks[22], (N_EVEN, W_A + W_B, D), (W_A + W_B) ** -0.5),
        'w_in_o': nrm(ks[23], (N_ODD, D, IN_O), D ** -0.5),
        'lb_logits': nrm(ks[24], (DEPTH, H_C * DK_C), 0.1),
        'g_norm_c': 1.0 + nrm(ks[25], (N_ODD, DV_C), 0.01),
        'w_out_o': nrm(ks[26], (N_ODD, W_C, D), W_C ** -0.5),
    }


def reference(x_prompt, x_sample, c_prompt, c_sample, cache_ckv, cache_kr, state_delta, state_conv, state_hgrn,
              w_ada, b_ada, g_pre, g_post, w_in_e, conv_w, a_log, dt_bias, g_norm_a, g_q, w_uq, g_kv, w_ukv,
              w_out_e, w_in_o, lb_logits, g_norm_c, w_out_o):
    Bp = x_prompt.shape[0]
    dt = x_prompt.dtype
    y_prompt, p_delta, p_conv, p_ckv, p_kr, p_hgrn = trunk(
        x_prompt, c_prompt,
        jnp.zeros((N_EVEN, Bp, 0, KV_LORA), dt), jnp.zeros((N_EVEN, Bp, 0, D_ROPE), dt),
        jnp.zeros((N_EVEN, Bp, H_A, DK_A, DV_A), dt), jnp.zeros((N_EVEN, Bp, CONV_W - 1, QKV_A), dt),
        jnp.zeros((N_ODD, Bp, H_C, DK_C, DV_C), dt),
        w_ada, b_ada, g_pre, g_post, w_in_e, conv_w, a_log, dt_bias, g_norm_a, g_q, w_uq, g_kv, w_ukv,
        w_out_e, w_in_o, lb_logits, g_norm_c, w_out_o)
    y_sample, s_delta, s_conv, s_ckv, s_kr, s_hgrn = trunk(
        x_sample, c_sample, cache_ckv, cache_kr, state_delta, state_conv, state_hgrn,
        w_ada, b_ada, g_pre, g_post, w_in_e, conv_w, a_log, dt_bias, g_norm_a, g_q, w_uq, g_kv, w_ukv,
        w_out_e, w_in_o, lb_logits, g_norm_c, w_out_o)
    return (y_prompt, y_sample, p_delta, p_conv, p_ckv, p_kr, p_hgrn, s_delta, s_conv, s_ckv, s_kr, s_hgrn)
```

```python
import functools
import math

import jax
import jax.numpy as jnp
from jax import lax
from jax.experimental import pallas as pl
from jax.experimental.pallas import tpu as pltpu

F32 = jnp.float32
BF16 = jnp.bfloat16

D = 1024
DEPTH = 4
CHUNK = 64
EPS = 1e-6
H_A, DK_A, DV_A, CONV_W = 4, 128, 128, 4
H_B, Q_LORA, KV_LORA, D_NOPE, D_ROPE, DV_B = 8, 384, 256, 64, 32, 64
ROPE_THETA = 10000.0
H_C, DK_C, DV_C = 8, 128, 128
W_A, W_B, W_C = H_A * DV_A, H_B * DV_B, H_C * DV_C
QKV_A = H_A * (2 * DK_A + DV_A)
LANES = 128
SUB = 16
EXP_CLAMP = 60.0
NEG_BIG = -1e30
VMEM_LIMIT = 56 * 1024 * 1024


def _cparams(n_axes):
    return pltpu.CompilerParams(dimension_semantics=("arbitrary",) * n_axes, vmem_limit_bytes=VMEM_LIMIT)


def _mm(a, b):
    return jnp.dot(a.astype(BF16), b.astype(BF16), preferred_element_type=F32)


def _mm_nt(a, b):
    return lax.dot_general(a.astype(BF16), b.astype(BF16), (((1,), (1,)), ((), ())), preferred_element_type=F32)


def _mm_tn(a, b):
    return lax.dot_general(a.astype(BF16), b.astype(BF16), (((0,), (0,)), ((), ())), preferred_element_type=F32)


def _split3(x):
    hi = x.astype(BF16)
    r1 = x - hi.astype(F32)
    mid = r1.astype(BF16)
    lo = (r1 - mid.astype(F32)).astype(BF16)
    return hi, mid, lo


def _cum_left(l01, x):
    hi, mid, lo = _split3(x)
    d = functools.partial(jnp.dot, preferred_element_type=F32)
    return d(l01, hi) + d(l01, mid) + d(l01, lo)


def _cum_right(x, r01):
    hi, mid, lo = _split3(x)
    d = functools.partial(jnp.dot, preferred_element_type=F32)
    return d(hi, r01) + d(mid, r01) + d(lo, r01)


def _sigmoid(x):
    return 1.0 / (1.0 + jnp.exp(-x))


def _silu(x):
    return x * _sigmoid(x)


def _softplus(x):
    return jnp.maximum(x, 0.0) + jnp.log(1.0 + jnp.exp(-jnp.abs(x)))


def _rms(x, g):
    return x * lax.rsqrt(jnp.mean(x * x, axis=-1, keepdims=True) + EPS) * g


def _prenorm(x, mod_ref, g_ref):
    return _rms(x, g_ref[...]) * (1.0 + mod_ref[0, 1:2, :]) + mod_ref[0, 0:1, :]


def _iota2(shape):
    return lax.broadcasted_iota(jnp.int32, shape, 0), lax.broadcasted_iota(jnp.int32, shape, 1)


def _ada_kernel(c_ref, w_ref, b_ref, o_ref):
    o_ref[0] = _mm(_silu(c_ref[...]), w_ref[0]) + b_ref[0]


def _ada(c, w_ada, b_ada):
    nb = c.shape[0]
    tn = 1024
    return pl.pallas_call(
        _ada_kernel,
        grid=(DEPTH, 3 * D // tn),
        in_specs=[pl.BlockSpec((nb, D), lambda l, j: (0, 0)),
                  pl.BlockSpec((1, D, tn), lambda l, j: (l, 0, j)),
                  pl.BlockSpec((1, 1, tn), lambda l, j: (l, 0, j))],
        out_specs=pl.BlockSpec((1, nb, tn), lambda l, j: (l, 0, j)),
        out_shape=jax.ShapeDtypeStruct((DEPTH, nb, 3 * D), F32),
        compiler_params=_cparams(2),
        name="ada_mod",
    )(c, w_ada, b_ada.reshape(DEPTH, 1, 3 * D))


NB_E = QKV_A + W_A + Q_LORA + KV_LORA + W_B
KR_LANE = 8


def _even_in_kernel(x_ref, mod_ref, gpre_ref, wbig_ref, wsm_ref, gq_ref, wq_ref, gkv_ref, wkv_ref, pm_ref,
                    cq_ref, sq_ref, ck_ref, sk_ref,
                    qkv_ref, za_ref, zb_ref, small_ref, ckv_ref, qp_ref, kp_ref, vp_ref):
    hb = _prenorm(x_ref[0], mod_ref, gpre_ref).astype(BF16)
    big = jnp.dot(hb, wbig_ref[...], preferred_element_type=F32)
    o = 0
    qkv_ref[0] = big[:, o:o + QKV_A]
    o += QKV_A
    za_ref[0] = big[:, o:o + W_A]
    o += W_A
    cq = big[:, o:o + Q_LORA]
    o += Q_LORA
    ckv = big[:, o:o + KV_LORA]
    o += KV_LORA
    zb_ref[0] = big[:, o:o + W_B]
    sm = jnp.dot(hb, wsm_ref[...], preferred_element_type=F32)
    small = sm[:, :LANES] * ck_ref[...] + sm[:, LANES:] * sk_ref[...]
    small_ref[0] = small
    q2 = _mm(_rms(cq, gq_ref[...]), wq_ref[...])
    nq = H_B * LANES
    cq_t = jnp.concatenate([cq_ref[...]] * H_B, axis=1)
    sq_t = jnp.concatenate([sq_ref[...]] * H_B, axis=1)
    qp_ref[0] = (q2[:, :nq] * cq_t + q2[:, nq:] * sq_t).astype(BF16)
    cn = _rms(ckv, gkv_ref[...])
    ckv_ref[0] = cn
    kv2 = _mm(cn, wkv_ref[...])
    kp_ref[0] = (kv2[:, :nq] + _mm(small, pm_ref[...])).astype(BF16)
    vp_ref[0] = kv2[:, nq:].astype(BF16)


def _even_in(x, mod, g_pre, wts, tabs, tm):
    B, T, _ = x.shape
    grid = (B, T // tm)
    full = lambda a: pl.BlockSpec(a.shape, lambda b, i: (0,) * a.ndim)
    tok = lambda n: pl.BlockSpec((1, tm, n), lambda b, i: (b, i, 0))
    tab = pl.BlockSpec((tm, LANES), lambda b, i: (i, 0))
    outs = [(QKV_A, F32), (W_A, F32), (W_B, F32), (LANES, F32), (KV_LORA, F32),
            (H_B * LANES, BF16), (H_B * LANES, BF16), (W_B, BF16)]
    return pl.pallas_call(
        _even_in_kernel,
        grid=grid,
        in_specs=[tok(D), pl.BlockSpec((1, 3, D), lambda b, i: (b, 0, 0)), full(g_pre),
                  full(wts["wbig"]), full(wts["wsm"]), full(wts["gq"]), full(wts["wq"]), full(wts["gkv"]),
                  full(wts["wkv"]), full(wts["pm"]), tab, tab, tab, tab],
        out_specs=[tok(n) for n, _ in outs],
        out_shape=[jax.ShapeDtypeStruct((B, T, n), dt) for n, dt in outs],
        compiler_params=_cparams(2),
        name="even_in",
    )(x, mod, g_pre, wts["wbig"], wts["wsm"], wts["gq"], wts["wq"], wts["gkv"], wts["wkv"], wts["pm"], *tabs)


def _unit_lower_inverse(m, r, c):
    eye = (r == c).astype(F32)
    same16 = (r >> 4) == (c >> 4)
    same32 = (r >> 5) == (c >> 5)
    md = jnp.where(same16, m, 0.0)
    m1 = jnp.where(same32 & jnp.logical_not(same16), m, 0.0)
    m2 = jnp.where(same32, 0.0, m)
    p = eye - md
    q = _mm(md, md)
    p = p + _mm(p, q)
    q = _mm(q, q)
    p = p + _mm(p, q)
    q = _mm(q, q)
    p = p + _mm(p, q)
    p = p - _mm(_mm(p, m1), p)
    return p - _mm(_mm(p, m2), p)


def _gdn_kernel(nc, qkv_ref, sm_ref, smt_ref, za_ref, convw_ref, cbuf_ref, s0_ref, rowc_ref, colc_ref, gn_ref,
                o_ref, sout_ref, cout_ref, xpad, act, s_scr):
    tc = nc * CHUNK

    @pl.when(pl.program_id(1) == 0)
    def _():
        xpad[0:8, :] = cbuf_ref[0]
        s_scr[...] = s0_ref[0]

    xpad[8:8 + tc, :] = qkv_ref[0]
    w = convw_ref[...]
    y = (w[0:1] * xpad[5:5 + tc, :] + w[1:2] * xpad[6:6 + tc, :] + w[2:3] * xpad[7:7 + tc, :]
         + w[3:4] * xpad[8:8 + tc, :])
    act[...] = _silu(y)
    tail = xpad[tc:tc + 8, :]
    cout_ref[0] = tail
    xpad[0:8, :] = tail

    r, c = _iota2((CHUNK, CHUNK))
    tril = c <= r
    strict = c < r
    tril_b = tril.astype(F32).astype(BF16)
    triu_b = (r <= c).astype(F32).astype(BF16)
    dtb_row, alog_row = rowc_ref[0:1, :], rowc_ref[1:2, :]
    dtb_col, alog_col = colc_ref[0, :, 0:CHUNK], colc_ref[1, :, 0:CHUNK]

    def chunk(ci, carry):
        r0 = pl.multiple_of(ci * CHUNK, CHUNK)
        rows = pl.ds(r0, CHUNK)
        sm = sm_ref[0, rows, :]
        beta_all = _sigmoid(sm)
        gc_all = _cum_left(tril_b, -jnp.exp(alog_row) * _softplus(sm + dtb_row))
        gct = _cum_right(-jnp.exp(alog_col) * _softplus(smt_ref[0, ci] + dtb_col), triu_b)
        for h in range(H_A):
            q = act[rows, h * DK_A:(h + 1) * DK_A]
            k = act[rows, H_A * DK_A + h * DK_A:H_A * DK_A + (h + 1) * DK_A]
            v = act[rows, 2 * H_A * DK_A + h * DV_A:2 * H_A * DK_A + (h + 1) * DV_A]
            q = q * lax.rsqrt(jnp.sum(q * q, axis=-1, keepdims=True) + EPS) * DK_A ** -0.5
            k = k * lax.rsqrt(jnp.sum(k * k, axis=-1, keepdims=True) + EPS)
            bcol = beta_all[:, h:h + 1]
            gcol = gc_all[:, H_A + h:H_A + h + 1]
            grow = gct[H_A + h:H_A + h + 1, :]
            dm = jnp.exp(jnp.where(tril, gcol - grow, NEG_BIG))
            kb = k * bcol
            m = _mm_nt(kb, k) * jnp.where(strict, dm, 0.0)
            tinv = _unit_lower_inverse(m, r, c)
            egc = jnp.exp(gcol)
            uw = _mm(tinv, jnp.concatenate([v * bcol, kb * egc], axis=1))
            u, wmat = uw[:, :DV_A], uw[:, DV_A:]
            gl = gcol[CHUNK - 1:CHUNK, :]
            attn = _mm_nt(q, k) * dm
            s = s_scr[h]
            ws = _mm(jnp.concatenate([wmat, q * egc], axis=0), s)
            v_new = u - ws[:CHUNK]
            o = ws[CHUNK:] + _mm(attn, v_new)
            s_scr[h] = s * jnp.exp(gl) + _mm_tn(k * jnp.exp(gl - gcol), v_new)
            o = _rms(o, gn_ref[...]) * _silu(za_ref[0, rows, h * DV_A:(h + 1) * DV_A])
            o_ref[0, rows, h * DV_A:(h + 1) * DV_A] = o
        return carry

    lax.fori_loop(0, nc, chunk, 0)
    sout_ref[0] = s_scr[...]


def _gdn(qkv, small, za, conv_w, conv_buf, s_delta, a_log, dt_bias, g_norm_a, nc):
    B, T, _ = qkv.shape
    tc = nc * CHUNK
    smt = small[..., :8].reshape(B, T // CHUNK, CHUNK, 8).transpose(0, 1, 3, 2)
    lane = jnp.zeros((LANES,), F32)
    rowc = jnp.stack([lane.at[H_A:2 * H_A].set(dt_bias), lane.at[H_A:2 * H_A].set(a_log)])
    col = jnp.zeros((8,), F32)
    colc = jnp.stack([col.at[H_A:2 * H_A].set(dt_bias), col.at[H_A:2 * H_A].set(a_log)])
    colc = jnp.broadcast_to(colc[:, :, None], (2, 8, LANES))
    cbuf = jnp.pad(conv_buf, ((0, 0), (8 - (CONV_W - 1), 0), (0, 0)))
    tok = lambda n: pl.BlockSpec((1, tc, n), lambda b, i: (b, i, 0))
    full = lambda a: pl.BlockSpec(a.shape, lambda b, i: (0,) * a.ndim)
    o, s_new, ctail = pl.pallas_call(
        functools.partial(_gdn_kernel, nc),
        grid=(B, T // tc),
        in_specs=[tok(QKV_A), tok(LANES), pl.BlockSpec((1, nc, 8, CHUNK), lambda b, i: (b, i, 0, 0)), tok(W_A),
                  full(conv_w), pl.BlockSpec((1, 8, QKV_A), lambda b, i: (b, 0, 0)),
                  pl.BlockSpec((1, H_A, DK_A, DV_A), lambda b, i: (b, 0, 0, 0)),
                  full(rowc), full(colc), full(g_norm_a)],
        out_specs=[tok(W_A), pl.BlockSpec((1, H_A, DK_A, DV_A), lambda b, i: (b, 0, 0, 0)),
                   pl.BlockSpec((1, 8, QKV_A), lambda b, i: (b, 0, 0))],
        out_shape=[jax.ShapeDtypeStruct((B, T, W_A), F32), jax.ShapeDtypeStruct((B, H_A, DK_A, DV_A), F32),
                   jax.ShapeDtypeStruct((B, 8, QKV_A), F32)],
        scratch_shapes=[pltpu.VMEM((tc + 8, QKV_A), F32), pltpu.VMEM((tc, QKV_A), F32),
                        pltpu.VMEM((H_A, DK_A, DV_A), F32)],
        compiler_params=_cparams(2),
        name="gdn",
    )(qkv, small, smt, za, conv_w, cbuf, s_delta, rowc, colc, g_norm_a)
    return o, s_new, ctail[:, 8 - (CONV_W - 1):, :]


def _attn_kernel(tq, tk, causal, q_ref, k_ref, v_ref, zb_ref, o_ref, m_scr, l_scr, acc_scr):
    i = pl.program_id(2)
    if causal:
        r, c = _iota2((tq, tk))
        mask = (c >> 6) <= (r >> 6)
    for hh in range(2):
        q = q_ref[0, :, hh * LANES:(hh + 1) * LANES]
        m_scr[hh] = jnp.full((tq, LANES), NEG_BIG, F32)
        l_scr[hh] = jnp.zeros((tq, LANES), F32)
        acc_scr[hh] = jnp.zeros((tq, LANES), F32)

        def tile(j, masked, hh=hh, q=q):
            rows = pl.ds(0, tk) if isinstance(j, int) else pl.ds(pl.multiple_of(j * tk, 8), tk)
            s = _mm_nt(q, k_ref[0, rows, hh * LANES:(hh + 1) * LANES])
            if masked:
                s = jnp.where(mask, s, NEG_BIG)
            m_prev = m_scr[hh]
            m_new = jnp.maximum(m_prev, jnp.max(s, axis=-1, keepdims=True))
            alpha = jnp.exp(m_prev - m_new)
            p = jnp.exp(s - m_new[:, 0:1])
            l_scr[hh] = alpha * l_scr[hh] + jnp.sum(p, axis=-1, keepdims=True)
            acc_scr[hh] = alpha * acc_scr[hh] + _mm(p, v_ref[0, rows, :])
            m_scr[hh] = m_new

        if causal:
            def body(j, carry):
                tile(j, False)
                return carry
            lax.fori_loop(0, i, body, 0)
            tile(i, True)
        else:
            tile(0, False)
    _, lane = _iota2((tq, LANES))
    o = jnp.where(lane < DV_B, acc_scr[0] / l_scr[0], acc_scr[1] / l_scr[1])
    o_ref[0] = o * _silu(zb_ref[0])


def _attention(qp, kp, vp, zb, tq, causal):
    B, T, _ = qp.shape
    tk_all = kp.shape[1]
    tk = tq if causal else tk_all
    return pl.pallas_call(
        functools.partial(_attn_kernel, tq, tk, causal),
        grid=(B, H_B // 2, T // tq),
        in_specs=[pl.BlockSpec((1, tq, 2 * LANES), lambda b, h, i: (b, i, h)),
                  pl.BlockSpec((1, tk_all, 2 * LANES), lambda b, h, i: (b, 0, h)),
                  pl.BlockSpec((1, tk_all, LANES), lambda b, h, i: (b, 0, h)),
                  pl.BlockSpec((1, tq, LANES), lambda b, h, i: (b, i, h))],
        out_specs=pl.BlockSpec((1, tq, LANES), lambda b, h, i: (b, i, h)),
        out_shape=jax.ShapeDtypeStruct((B, T, W_B), F32),
        scratch_shapes=[pltpu.VMEM((2, tq, LANES), F32)] * 3,
        compiler_params=_cparams(3),
        name="attn",
    )(qp, kp, vp, zb)


def _kv_past_kernel(c_ref, kr_ref, wkv_ref, pm_ref, kp_ref, vp_ref):
    kv2 = _mm(c_ref[0], wkv_ref[...])
    nq = H_B * LANES
    kp_ref[0] = (kv2[:, :nq] + _mm(kr_ref[0], pm_ref[...])).astype(BF16)
    vp_ref[0] = kv2[:, nq:].astype(BF16)


def _kv_past(ckv_past, kr_past, wts, tm):
    B, P, _ = ckv_past.shape
    krp = jnp.pad(kr_past, ((0, 0), (0, 0), (KR_LANE, LANES - KR_LANE - D_ROPE)))
    tok = lambda n: pl.BlockSpec((1, tm, n), lambda b, i: (b, i, 0))
    full = lambda a: pl.BlockSpec(a.shape, lambda b, i: (0,) * a.ndim)
    return pl.pallas_call(
        _kv_past_kernel,
        grid=(B, P // tm),
        in_specs=[tok(KV_LORA), tok(LANES), full(wts["wkv"]), full(wts["pm"])],
        out_specs=[tok(H_B * LANES), tok(W_B)],
        out_shape=[jax.ShapeDtypeStruct((B, P, H_B * LANES), BF16), jax.ShapeDtypeStruct((B, P, W_B), BF16)],
        compiler_params=_cparams(2),
        name="kv_past",
    )(ckv_past, krp, wts["wkv"], wts["pm"])


def _out_kernel(n_in, *refs):
    ins = refs[:n_in]
    ws = refs[n_in:2 * n_in]
    x_ref, mod_ref, gpost_ref, o_ref = refs[2 * n_in:]
    acc = _mm(ins[0][0], ws[0][...])
    for a, w in zip(ins[1:], ws[1:]):
        acc = acc + _mm(a[0], w[...])
    o_ref[0] = x_ref[0] + mod_ref[0, 2:3, :] * _rms(acc, gpost_ref[...])


def _out_proj(ins, ws, x, mod, g_post, tm):
    B, T, _ = x.shape
    tok = lambda n: pl.BlockSpec((1, tm, n), lambda b, i: (b, i, 0))
    full = lambda a: pl.BlockSpec(a.shape, lambda b, i: (0,) * a.ndim)
    return pl.pallas_call(
        functools.partial(_out_kernel, len(ins)),
        grid=(B, T // tm),
        in_specs=[tok(a.shape[-1]) for a in ins] + [full(w) for w in ws]
        + [tok(D), pl.BlockSpec((1, 3, D), lambda b, i: (b, 0, 0)), full(g_post)],
        out_specs=tok(D),
        out_shape=jax.ShapeDtypeStruct((B, T, D), F32),
        compiler_params=_cparams(2),
        name="out_proj",
    )(*ins, *ws, x, mod, g_post)


def _odd_in_kernel(layer, x_ref, mod_ref, gpre_ref, w_ref, lbl_ref, qs_ref, lf_ref, iv_ref, z_ref):
    hb = _prenorm(x_ref[0], mod_ref, gpre_ref).astype(BF16)
    proj = jnp.dot(hb, w_ref[...], preferred_element_type=F32)
    n = H_C * DK_C
    lg = lbl_ref[...]
    e = jnp.exp(lg - jnp.max(lg, axis=0, keepdims=True))
    p = e / jnp.sum(e, axis=0, keepdims=True)
    lb = jnp.sum(p[1:layer + 1], axis=0, keepdims=True)
    fg = lb + (1.0 - lb) * _sigmoid(proj[:, n:2 * n])
    qs_ref[0] = _silu(proj[:, :n])
    lf_ref[0] = jnp.log(fg)
    iv_ref[0] = proj[:, 2 * n:2 * n + W_C]
    z_ref[0] = proj[:, 2 * n + W_C:]


def _odd_in(x, mod, g_pre, w, lb_logits, layer, tm):
    B, T, _ = x.shape
    tok = lambda n: pl.BlockSpec((1, tm, n), lambda b, i: (b, i, 0))
    full = lambda a: pl.BlockSpec(a.shape, lambda b, i: (0,) * a.ndim)
    return pl.pallas_call(
        functools.partial(_odd_in_kernel, layer),
        grid=(B, T // tm),
        in_specs=[tok(D), pl.BlockSpec((1, 3, D), lambda b, i: (b, 0, 0)), full(g_pre), full(w), full(lb_logits)],
        out_specs=[tok(W_C)] * 4,
        out_shape=[jax.ShapeDtypeStruct((B, T, W_C), F32)] * 4,
        compiler_params=_cparams(2),
        name="odd_in",
    )(x, mod, g_pre, w, lb_logits)


def _hgrn_kernel(nc, qs_ref, lf_ref, iv_ref, z_ref, s0_ref, gn_ref, o_ref, sout_ref, s_scr):
    @pl.when(pl.program_id(1) == 0)
    def _():
        s_scr[...] = s0_ref[0]

    nsub = CHUNK // SUB
    r2, c2 = _iota2((2 * CHUNK, CHUNK))
    rr = r2 & (CHUNK - 1)
    cum_l = ((c2 <= rr) & ((r2 < CHUNK) | ((c2 >> 4) == (rr >> 4)))).astype(F32).astype(BF16)
    rm, cm = _iota2((CHUNK, nsub * CHUNK))
    pmask = ((rm >> 4) == (cm >> 6)) & ((cm & (CHUNK - 1)) <= rm)

    def chunk(ci, carry):
        rows = pl.ds(pl.multiple_of(ci * CHUNK, CHUNK), CHUNK)
        for h in range(H_C):
            lanes = slice(h * DK_C, (h + 1) * DK_C)
            lf = lf_ref[0, rows, lanes]
            qs = qs_ref[0, rows, lanes]
            iv = iv_ref[0, rows, lanes]
            gw = _cum_left(cum_l, lf)
            g, wloc = gw[:CHUNK], gw[CHUNK:]
            k = 1.0 - jnp.exp(lf)
            segs = [k * jnp.exp(jnp.minimum(-g, EXP_CLAMP))]
            for si in range(1, nsub):
                segs.append(k * jnp.exp(jnp.minimum(g[si * SUB - 1:si * SUB, :] - g, EXP_CLAMP)))
            sc = _mm_nt(qs * jnp.exp(wloc), jnp.concatenate(segs, axis=0))
            p = jnp.where(pmask, sc, 0.0)
            st = s_scr[h]
            o = _mm(p, jnp.concatenate([iv] * nsub, axis=0)) + _mm_nt(qs * jnp.exp(g), st)
            gl = g[CHUNK - 1:CHUNK, :]
            s_scr[h] = st * jnp.exp(gl) + _mm_tn(iv, k * jnp.exp(gl - g))
            o_ref[0, rows, lanes] = _rms(o, gn_ref[...]) * _silu(z_ref[0, rows, lanes])
        return carry

    lax.fori_loop(0, nc, chunk, 0)
    sout_ref[0] = s_scr[...]


def _hgrn(qs, lf, iv, z, s_hgrn, g_norm_c, nc):
    B, T, _ = qs.shape
    tc = nc * CHUNK
    tok = pl.BlockSpec((1, tc, W_C), lambda b, i: (b, i, 0))
    st = pl.BlockSpec((1, H_C, DV_C, DK_C), lambda b, i: (b, 0, 0, 0))
    o, s_new = pl.pallas_call(
        functools.partial(_hgrn_kernel, nc),
        grid=(B, T // tc),
        in_specs=[tok, tok, tok, tok, st, pl.BlockSpec(g_norm_c.shape, lambda b, i: (0, 0))],
        out_specs=[tok, st],
        out_shape=[jax.ShapeDtypeStruct((B, T, W_C), F32), jax.ShapeDtypeStruct((B, H_C, DV_C, DK_C), F32)],
        scratch_shapes=[pltpu.VMEM((H_C, DV_C, DK_C), F32)],
        compiler_params=_cparams(2),
        name="hgrn",
    )(qs, lf, iv, z, jnp.swapaxes(s_hgrn, -1, -2), g_norm_c)
    return o, jnp.swapaxes(s_new, -1, -2)


def _even_weights(w_in, g_q, w_uq, g_kv, w_ukv):
    o_b, o_a, o_za = QKV_A, QKV_A + H_A, QKV_A + 2 * H_A
    o_cq = o_za + W_A
    o_ckv = o_cq + Q_LORA
    o_kr = o_ckv + KV_LORA
    o_zb = o_kr + D_ROPE
    half = D_ROPE // 2
    wbig = jnp.concatenate([w_in[:, :QKV_A], w_in[:, o_za:o_cq], w_in[:, o_cq:o_ckv], w_in[:, o_ckv:o_kr],
                            w_in[:, o_zb:]], axis=1)
    kr = w_in[:, o_kr:o_zb]
    z = lambda n: jnp.zeros((D, n), F32)
    wsm = jnp.concatenate([w_in[:, o_b:o_za], kr, z(LANES - KR_LANE - D_ROPE),
                           z(KR_LANE), -kr[:, half:], kr[:, :half], z(LANES - KR_LANE - D_ROPE)], axis=1)
    w3 = w_uq.reshape(Q_LORA, H_B, D_NOPE + D_ROPE)
    nope, r1, r2 = w3[..., :D_NOPE], w3[..., D_NOPE:D_NOPE + half], w3[..., D_NOPE + half:]
    zq = lambda n: jnp.zeros((Q_LORA, H_B, n), F32)
    pad = LANES - D_NOPE - D_ROPE
    wq = jnp.concatenate([jnp.concatenate([nope, r1, r2, zq(pad)], -1).reshape(Q_LORA, H_B * LANES),
                          jnp.concatenate([zq(D_NOPE), -r2, r1, zq(pad)], -1).reshape(Q_LORA, H_B * LANES)], axis=1)
    k3 = w_ukv.reshape(KV_LORA, H_B, D_NOPE + DV_B)
    wk = jnp.concatenate([k3[..., :D_NOPE], jnp.zeros((KV_LORA, H_B, LANES - D_NOPE), F32)], -1)
    wkv = jnp.concatenate([wk.reshape(KV_LORA, H_B * LANES), k3[..., D_NOPE:].reshape(KV_LORA, W_B)], axis=1)
    j = jnp.arange(D_ROPE)
    pm = jnp.zeros((LANES, H_B, LANES), F32).at[KR_LANE + j, :, D_NOPE + j].set(1.0).reshape(LANES, H_B * LANES)
    return dict(wbig=wbig.astype(BF16), wsm=wsm.astype(BF16), wq=wq.astype(BF16), wkv=wkv.astype(BF16),
                pm=pm.astype(BF16), gq=g_q.reshape(1, Q_LORA), gkv=g_kv.reshape(1, KV_LORA))


def _rope_tables(pos):
    half = D_ROPE // 2
    inv = ROPE_THETA ** (-jnp.arange(half, dtype=F32) / half)
    ang = pos.astype(F32)[:, None] * inv[None, :]
    cos, sin = jnp.cos(ang), jnp.sin(ang)
    t = pos.shape[0]
    one, zero = jnp.ones, jnp.zeros
    scale = (D_NOPE + D_ROPE) ** -0.5
    pad_q = LANES - D_NOPE - D_ROPE
    pad_k = LANES - KR_LANE - D_ROPE
    cq = scale * jnp.concatenate([one((t, D_NOPE), F32), cos, cos, zero((t, pad_q), F32)], axis=1)
    sq = scale * jnp.concatenate([zero((t, D_NOPE), F32), sin, sin, zero((t, pad_q), F32)], axis=1)
    ck = jnp.concatenate([one((t, KR_LANE), F32), cos, cos, zero((t, pad_k), F32)], axis=1)
    sk = jnp.concatenate([zero((t, KR_LANE), F32), sin, sin, zero((t, pad_k), F32)], axis=1)
    return cq, sq, ck, sk


def _tiles(T):
    tm = min(256, T)
    nc = min(4, T // CHUNK)
    return tm, nc


def _trunk(x, mod, ckv_past, kr_past, s_delta, s_conv, s_hgrn, prm, tiles=None):
    B, T, _ = x.shape
    tm, nc = tiles or _tiles(T)
    past = 0 if ckv_past is None else ckv_past.shape[2]
    tabs = _rope_tables(past + jnp.arange(T, dtype=jnp.int32))
    nd, ncv, nk, nr, nh = [], [], [], [], []
    for l in range(DEPTH):
        m = mod[l].reshape(B, 3, D)
        g_pre, g_post = prm["g_pre"][l].reshape(1, D), prm["g_post"][l].reshape(1, D)
        if l % 2 == 0:
            e = l // 2
            wts = prm["even"][e]
            qkv, za, zb, small, ckv, qp, kp, vp = _even_in(x, m, g_pre, wts, tabs, tm)
            o_a, s_new, conv_new = _gdn(qkv, small, za, prm["conv_w"][e], s_conv[e], s_delta[e], prm["a_log"][e],
                                        prm["dt_bias"][e], prm["g_norm_a"][e].reshape(1, DV_A), nc)
            if past:
                kpast, vpast = _kv_past(ckv_past[e], kr_past[e], wts, min(512, past))
                kp = jnp.concatenate([kpast, kp], axis=1)
                vp = jnp.concatenate([vpast, vp], axis=1)
            o_b = _attention(qp, kp, vp, zb, tm if not past else T, causal=not past)
            w_out = prm["w_out_e"][e]
            x = _out_proj([o_a, o_b], [w_out[:W_A], w_out[W_A:]], x, m, g_post, tm)
            nd.append(s_new)
            ncv.append(conv_new)
            nk.append(ckv)
            nr.append(small[..., KR_LANE:KR_LANE + D_ROPE])
        else:
            j = l // 2
            qs, lf, iv, z = _odd_in(x, m, g_pre, prm["w_in_o"][j], prm["lb_logits"], l, tm)
            o, s_new = _hgrn(qs, lf, iv, z, s_hgrn[j], prm["g_norm_c"][j].reshape(1, DV_C), nc)
            x = _out_proj([o], [prm["w_out_o"][j]], x, m, g_post, tm)
            nh.append(s_new)
    return (x, jnp.stack(nd), jnp.stack(ncv), jnp.stack(nk), jnp.stack(nr), jnp.stack(nh))


def _forward(x_prompt, x_sample, c_prompt, c_sample, cache_ckv, cache_kr, state_delta, state_conv, state_hgrn,
             w_ada, b_ada, g_pre, g_post, w_in_e, conv_w, a_log, dt_bias, g_norm_a, g_q, w_uq, g_kv, w_ukv,
             w_out_e, w_in_o, lb_logits, g_norm_c, w_out_o, tiles_prompt=None, tiles_sample=None):
    bp, bs = x_prompt.shape[0], x_sample.shape[0]
    n_even, n_odd = w_in_e.shape[0], w_in_o.shape[0]
    mod = _ada(jnp.concatenate([c_prompt, c_sample], axis=0), w_ada, b_ada)
    prm = dict(
        g_pre=g_pre, g_post=g_post, conv_w=conv_w, a_log=a_log, dt_bias=dt_bias, g_norm_a=g_norm_a,
        g_norm_c=g_norm_c, lb_logits=lb_logits,
        even=[_even_weights(w_in_e[e], g_q[e], w_uq[e], g_kv[e], w_ukv[e]) for e in range(n_even)],
        w_out_e=w_out_e.astype(BF16), w_in_o=w_in_o.astype(BF16), w_out_o=w_out_o.astype(BF16))
    zeros = lambda *s: jnp.zeros(s, F32)
    outs_p = _trunk(x_prompt, mod[:, :bp], None, None, zeros(n_even, bp, H_A, DK_A, DV_A),
                    zeros(n_even, bp, CONV_W - 1, QKV_A), zeros(n_odd, bp, H_C, DK_C, DV_C), prm, tiles_prompt)
    outs_s = _trunk(x_sample, mod[:, bp:], cache_ckv, cache_kr, state_delta, state_conv, state_hgrn, prm,
                    tiles_sample)
    return (outs_p[0], outs_s[0]) + outs_p[1:] + outs_s[1:]


def kernel(x_prompt, x_sample, c_prompt, c_sample, cache_ckv, cache_kr, state_delta, state_conv, state_hgrn,
           w_ada, b_ada, g_pre, g_post, w_in_e, conv_w, a_log, dt_bias, g_norm_a, g_q, w_uq, g_kv, w_ukv,
           w_out_e, w_in_o, lb_logits, g_norm_c, w_out_o):
    return _forward(x_prompt, x_sample, c_prompt, c_sample, cache_ckv, cache_kr, state_delta, state_conv,
                    state_hgrn, w_ada, b_ada, g_pre, g_post, w_in_e, conv_w, a_log, dt_bias, g_norm_a, g_q, w_uq,
                    g_kv, w_ukv, w_out_e, w_in_o, lb_logits, g_norm_c, w_out_o)
```

```python
import functools
import math

import jax
import jax.numpy as jnp
from jax import lax
from jax.experimental import pallas as pl
from jax.experimental.pallas import tpu as pltpu

F32 = jnp.float32
BF16 = jnp.bfloat16

D = 1024
DEPTH = 4
CHUNK = 64
EPS = 1e-6
H_A, DK_A, DV_A, CONV_W = 4, 128, 128, 4
H_B, Q_LORA, KV_LORA, D_NOPE, D_ROPE, DV_B = 8, 384, 256, 64, 32, 64
ROPE_THETA = 10000.0
H_C, DK_C, DV_C = 8, 128, 128
W_A, W_B, W_C = H_A * DV_A, H_B * DV_B, H_C * DV_C
QKV_A = H_A * (2 * DK_A + DV_A)
LANES = 128
SUB = 16
EXP_CLAMP = 60.0
NEG_BIG = -1e30
VMEM_LIMIT = 56 * 1024 * 1024


def _cparams(n_axes):
    return pltpu.CompilerParams(dimension_semantics=("arbitrary",) * n_axes, vmem_limit_bytes=VMEM_LIMIT)


def _mm(a, b):
    return jnp.dot(a.astype(BF16), b.astype(BF16), preferred_element_type=F32)


def _mm_nt(a, b):
    return lax.dot_general(a.astype(BF16), b.astype(BF16), (((1,), (1,)), ((), ())), preferred_element_type=F32)


def _mm_tn(a, b):
    return lax.dot_general(a.astype(BF16), b.astype(BF16), (((0,), (0,)), ((), ())), preferred_element_type=F32)


def _split3(x):
    hi = x.astype(BF16)
    r1 = x - hi.astype(F32)
    mid = r1.astype(BF16)
    lo = (r1 - mid.astype(F32)).astype(BF16)
    return hi, mid, lo


def _cum_left(l01, x):
    hi, mid, lo = _split3(x)
    d = functools.partial(jnp.dot, preferred_element_type=F32)
    return d(l01, hi) + d(l01, mid) + d(l01, lo)


def _cum_right(x, r01):
    hi, mid, lo = _split3(x)
    d = functools.partial(jnp.dot, preferred_element_type=F32)
    return d(hi, r01) + d(mid, r01) + d(lo, r01)


def _sigmoid(x):
    return 1.0 / (1.0 + jnp.exp(-x))


def _silu(x):
    return x * _sigmoid(x)


def _softplus(x):
    return jnp.maximum(x, 0.0) + jnp.log(1.0 + jnp.exp(-jnp.abs(x)))


def _rms(x, g):
    return x * lax.rsqrt(jnp.mean(x * x, axis=-1, keepdims=True) + EPS) * g


def _prenorm(x, mod_ref, g_ref):
    return _rms(x, g_ref[...]) * (1.0 + mod_ref[0, 1:2, :]) + mod_ref[0, 0:1, :]


def _iota2(shape):
    return lax.broadcasted_iota(jnp.int32, shape, 0), lax.broadcasted_iota(jnp.int32, shape, 1)


def _ada_kernel(c_ref, w_ref, b_ref, o_ref):
    o_ref[0] = _mm(_silu(c_ref[...]), w_ref[0]) + b_ref[0]


def _ada(c, w_ada, b_ada):
    nb = c.shape[0]
    tn = 1024
    return pl.pallas_call(
        _ada_kernel,
        grid=(DEPTH, 3 * D // tn),
        in_specs=[pl.BlockSpec((nb, D), lambda l, j: (0, 0)),
                  pl.BlockSpec((1, D, tn), lambda l, j: (l, 0, j)),
                  pl.BlockSpec((1, 1, tn), lambda l, j: (l, 0, j))],
        out_specs=pl.BlockSpec((1, nb, tn), lambda l, j: (l, 0, j)),
        out_shape=jax.ShapeDtypeStruct((DEPTH, nb, 3 * D), F32),
        compiler_params=_cparams(2),
        name="ada_mod",
    )(c, w_ada, b_ada.reshape(DEPTH, 1, 3 * D))


NB_E = QKV_A + W_A + Q_LORA + KV_LORA + W_B
KR_LANE = 8


def _even_in_kernel(x_ref, mod_ref, gpre_ref, wbig_ref, wsm_ref, gq_ref, wq_ref, gkv_ref, wkv_ref, pm_ref,
                    cq_ref, sq_ref, ck_ref, sk_ref,
                    qkv_ref, za_ref, zb_ref, small_ref, ckv_ref, qp_ref, kp_ref, vp_ref):
    hb = _prenorm(x_ref[0], mod_ref, gpre_ref).astype(BF16)
    big = jnp.dot(hb, wbig_ref[...], preferred_element_type=F32)
    o = 0
    qkv_ref[0] = big[:, o:o + QKV_A]
    o += QKV_A
    za_ref[0] = big[:, o:o + W_A]
    o += W_A
    cq = big[:, o:o + Q_LORA]
    o += Q_LORA
    ckv = big[:, o:o + KV_LORA]
    o += KV_LORA
    zb_ref[0] = big[:, o:o + W_B]
    sm = jnp.dot(hb, wsm_ref[...], preferred_element_type=F32)
    small = sm[:, :LANES] * ck_ref[...] + sm[:, LANES:] * sk_ref[...]
    small_ref[0] = small
    q2 = _mm(_rms(cq, gq_ref[...]), wq_ref[...])
    nq = H_B * LANES
    cq_t = jnp.concatenate([cq_ref[...]] * H_B, axis=1)
    sq_t = jnp.concatenate([sq_ref[...]] * H_B, axis=1)
    qp_ref[0] = (q2[:, :nq] * cq_t + q2[:, nq:] * sq_t).astype(BF16)
    cn = _rms(ckv, gkv_ref[...])
    ckv_ref[0] = cn
    kv2 = _mm(cn, wkv_ref[...])
    kp_ref[0] = (kv2[:, :nq] + _mm(small, pm_ref[...])).astype(BF16)
    vp_ref[0] = kv2[:, nq:].astype(BF16)


def _even_in(x, mod, g_pre, wts, tabs, tm):
    B, T, _ = x.shape
    grid = (B, T // tm)
    full = lambda a: pl.BlockSpec(a.shape, lambda b, i: (0,) * a.ndim)
    tok = lambda n: pl.BlockSpec((1, tm, n), lambda b, i: (b, i, 0))
    tab = pl.BlockSpec((tm, LANES), lambda b, i: (i, 0))
    outs = [(QKV_A, F32), (W_A, F32), (W_B, F32), (LANES, F32), (KV_LORA, F32),
            (H_B * LANES, BF16), (H_B * LANES, BF16), (W_B, BF16)]
    return pl.pallas_call(
        _even_in_kernel,
        grid=grid,
        in_specs=[tok(D), pl.BlockSpec((1, 3, D), lambda b, i: (b, 0, 0)), full(g_pre),
                  full(wts["wbig"]), full(wts["wsm"]), full(wts["gq"]), full(wts["wq"]), full(wts["gkv"]),
                  full(wts["wkv"]), full(wts["pm"]), tab, tab, tab, tab],
        out_specs=[tok(n) for n, _ in outs],
        out_shape=[jax.ShapeDtypeStruct((B, T, n), dt) for n, dt in outs],
        compiler_params=_cparams(2),
        name="even_in",
    )(x, mod, g_pre, wts["wbig"], wts["wsm"], wts["gq"], wts["wq"], wts["gkv"], wts["wkv"], wts["pm"], *tabs)


def _unit_lower_inverse(ms, r, c):
    eye = (r == c).astype(F32)
    same16 = (r >> 4) == (c >> 4)
    same32 = (r >> 5) == (c >> 5)
    off16 = same32 & jnp.logical_not(same16)
    md = [jnp.where(same16, m, 0.0) for m in ms]
    m1 = [jnp.where(off16, m, 0.0).astype(BF16) for m in ms]
    m2 = [jnp.where(same32, 0.0, m).astype(BF16) for m in ms]
    p = [eye - x for x in md]
    q = [_mm(x, x) for x in md]
    for step in range(3):
        p = [a + _mm(a, b) for a, b in zip(p, q)]
        if step < 2:
            q = [_mm(b, b) for b in q]
    for mk in (m1, m2):
        t = [_mm(a, b) for a, b in zip(p, mk)]
        p = [a - _mm(b, a) for a, b in zip(p, t)]
    return p


def _gdn_kernel(nc, qkv_ref, sm_ref, smt_ref, za_ref, convw_ref, cbuf_ref, s0_ref, rowc_ref, colc_ref, gn_ref,
                o_ref, sout_ref, cout_ref, xpad, act, s_scr):
    tc = nc * CHUNK

    @pl.when(pl.program_id(1) == 0)
    def _():
        xpad[0:8, :] = cbuf_ref[0]
        s_scr[...] = s0_ref[0]

    xpad[8:8 + tc, :] = qkv_ref[0]
    w = convw_ref[...]
    y = (w[0:1] * xpad[5:5 + tc, :] + w[1:2] * xpad[6:6 + tc, :] + w[2:3] * xpad[7:7 + tc, :]
         + w[3:4] * xpad[8:8 + tc, :])
    act[...] = _silu(y)
    tail = xpad[tc:tc + 8, :]
    cout_ref[0] = tail
    xpad[0:8, :] = tail

    r, c = _iota2((CHUNK, CHUNK))
    tril = c <= r
    strict = c < r
    tril_b = tril.astype(F32).astype(BF16)
    triu_b = (r <= c).astype(F32).astype(BF16)
    dtb_row, alog_row = rowc_ref[0:1, :], rowc_ref[1:2, :]
    dtb_col, alog_col = colc_ref[0, :, 0:CHUNK], colc_ref[1, :, 0:CHUNK]

    items = [(ci, h) for ci in range(nc) for h in range(H_A)]
    rows = [slice(ci * CHUNK, (ci + 1) * CHUNK) for ci in range(nc)]
    sm = sm_ref[0]
    beta_all = _sigmoid(sm)
    g_all = -jnp.exp(alog_row) * _softplus(sm + dtb_row)
    gc_all = [_cum_left(tril_b, g_all[rows[ci]]) for ci in range(nc)]
    gct = [_cum_right(-jnp.exp(alog_col) * _softplus(smt_ref[0, ci] + dtb_col), triu_b) for ci in range(nc)]

    def l2n(x):
        return x * lax.rsqrt(jnp.sum(x * x, axis=-1, keepdims=True) + EPS)

    qs = [l2n(act[rows[ci], h * DK_A:(h + 1) * DK_A]) * DK_A ** -0.5 for ci, h in items]
    ks = [l2n(act[rows[ci], (H_A + h) * DK_A:(H_A + h + 1) * DK_A]) for ci, h in items]
    vs = [act[rows[ci], 2 * H_A * DK_A + h * DV_A:2 * H_A * DK_A + (h + 1) * DV_A] for ci, h in items]
    bcol = [beta_all[rows[ci], h:h + 1] for ci, h in items]
    gcol = [gc_all[ci][:, H_A + h:H_A + h + 1] for ci, h in items]
    dm = [jnp.exp(jnp.where(tril, gcol[n] - gct[ci][H_A + h:H_A + h + 1, :], NEG_BIG))
          for n, (ci, h) in enumerate(items)]
    kb = [k * b for k, b in zip(ks, bcol)]
    kbf = [k.astype(BF16) for k in ks]
    ms = [_mm_nt(a, b) * jnp.where(strict, d, 0.0) for a, b, d in zip(kb, kbf, dm)]
    tinv = _unit_lower_inverse(ms, r, c)
    egc = [jnp.exp(g) for g in gcol]
    uw = [_mm(t, jnp.concatenate([v * b, x * e], axis=1)) for t, v, b, x, e in zip(tinv, vs, bcol, kb, egc)]
    attn = [(_mm_nt(q, k) * d).astype(BF16) for q, k, d in zip(qs, kbf, dm)]
    gl = [g[CHUNK - 1:CHUNK, :] for g in gcol]
    wq = [jnp.concatenate([x[:, DV_A:], q * e], axis=0).astype(BF16) for x, q, e in zip(uw, qs, egc)]
    kgt = [jnp.transpose(k * jnp.exp(l - g)).astype(BF16) for k, l, g in zip(ks, gl, gcol)]
    egl = [jnp.exp(l) for l in gl]

    s = [s_scr[h] for h in range(H_A)]
    for ci in range(nc):
        idx = [ci * H_A + h for h in range(H_A)]
        ws = [_mm(wq[n], s[h]) for h, n in enumerate(idx)]
        v_new = [uw[n][:, :DV_A] - ws[h][:CHUNK] for h, n in enumerate(idx)]
        o = [ws[h][CHUNK:] + _mm(attn[n], v_new[h]) for h, n in enumerate(idx)]
        s = [s[h] * egl[n] + _mm(kgt[n], v_new[h]) for h, n in enumerate(idx)]
        for h in range(H_A):
            lanes = slice(h * DV_A, (h + 1) * DV_A)
            o_ref[0, rows[ci], lanes] = _rms(o[h], gn_ref[...]) * _silu(za_ref[0, rows[ci], lanes])
    for h in range(H_A):
        s_scr[h] = s[h]
    sout_ref[0] = s_scr[...]


def _gdn(qkv, small, za, conv_w, conv_buf, s_delta, a_log, dt_bias, g_norm_a, nc):
    B, T, _ = qkv.shape
    tc = nc * CHUNK
    smt = small[..., :8].reshape(B, T // CHUNK, CHUNK, 8).transpose(0, 1, 3, 2)
    lane = jnp.zeros((LANES,), F32)
    rowc = jnp.stack([lane.at[H_A:2 * H_A].set(dt_bias), lane.at[H_A:2 * H_A].set(a_log)])
    col = jnp.zeros((8,), F32)
    colc = jnp.stack([col.at[H_A:2 * H_A].set(dt_bias), col.at[H_A:2 * H_A].set(a_log)])
    colc = jnp.broadcast_to(colc[:, :, None], (2, 8, LANES))
    cbuf = jnp.pad(conv_buf, ((0, 0), (8 - (CONV_W - 1), 0), (0, 0)))
    tok = lambda n: pl.BlockSpec((1, tc, n), lambda b, i: (b, i, 0))
    full = lambda a: pl.BlockSpec(a.shape, lambda b, i: (0,) * a.ndim)
    o, s_new, ctail = pl.pallas_call(
        functools.partial(_gdn_kernel, nc),
        grid=(B, T // tc),
        in_specs=[tok(QKV_A), tok(LANES), pl.BlockSpec((1, nc, 8, CHUNK), lambda b, i: (b, i, 0, 0)), tok(W_A),
                  full(conv_w), pl.BlockSpec((1, 8, QKV_A), lambda b, i: (b, 0, 0)),
                  pl.BlockSpec((1, H_A, DK_A, DV_A), lambda b, i: (b, 0, 0, 0)),
                  full(rowc), full(colc), full(g_norm_a)],
        out_specs=[tok(W_A), pl.BlockSpec((1, H_A, DK_A, DV_A), lambda b, i: (b, 0, 0, 0)),
                   pl.BlockSpec((1, 8, QKV_A), lambda b, i: (b, 0, 0))],
        out_shape=[jax.ShapeDtypeStruct((B, T, W_A), F32), jax.ShapeDtypeStruct((B, H_A, DK_A, DV_A), F32),
                   jax.ShapeDtypeStruct((B, 8, QKV_A), F32)],
        scratch_shapes=[pltpu.VMEM((tc + 8, QKV_A), F32), pltpu.VMEM((tc, QKV_A), F32),
                        pltpu.VMEM((H_A, DK_A, DV_A), F32)],
        compiler_params=_cparams(2),
        name="gdn",
    )(qkv, small, smt, za, conv_w, cbuf, s_delta, rowc, colc, g_norm_a)
    return o, s_new, ctail[:, 8 - (CONV_W - 1):, :]


def _attn_kernel(tq, tk, causal, q_ref, k_ref, v_ref, zb_ref, o_ref, m_scr, l_scr, acc_scr):
    i = pl.program_id(2)
    heads = (0, 1)
    if causal:
        r, c = _iota2((tq, tk))
        mask = (c >> 6) <= (r >> 6)
    qs = [q_ref[0, :, hh * LANES:(hh + 1) * LANES] for hh in heads]
    m_scr[...] = jnp.full((2, tq, LANES), NEG_BIG, F32)
    l_scr[...] = jnp.zeros((2, tq, LANES), F32)
    acc_scr[...] = jnp.zeros((2, tq, LANES), F32)
    rep = tk // LANES if tk % LANES == 0 else 0

    def tile(j, masked):
        rows = pl.ds(0, tk) if isinstance(j, int) else pl.ds(pl.multiple_of(j * tk, tk), tk)
        s = [_mm_nt(qs[hh], k_ref[0, rows, hh * LANES:(hh + 1) * LANES]) for hh in heads]
        if masked:
            s = [jnp.where(mask, x, NEG_BIG) for x in s]
        m_prev = [m_scr[hh] for hh in heads]
        m_new = [jnp.maximum(m_prev[hh], jnp.max(s[hh], axis=-1, keepdims=True)) for hh in heads]
        alpha = [jnp.exp(m_prev[hh] - m_new[hh]) for hh in heads]
        if rep:
            p = [jnp.exp(s[hh] - jnp.concatenate([m_new[hh]] * rep, axis=1)) for hh in heads]
        else:
            p = [jnp.exp(s[hh] - m_new[hh][:, 0:1]) for hh in heads]
        v = v_ref[0, rows, :]
        pv = [_mm(p[hh], v) for hh in heads]
        for hh in heads:
            l_scr[hh] = alpha[hh] * l_scr[hh] + jnp.sum(p[hh], axis=-1, keepdims=True)
            acc_scr[hh] = alpha[hh] * acc_scr[hh] + pv[hh]
            m_scr[hh] = m_new[hh]

    if causal:
        def body(j, carry):
            tile(j, False)
            return carry
        lax.fori_loop(0, i, body, 0)
        tile(i, True)
    else:
        tile(0, False)
    _, lane = _iota2((tq, LANES))
    o = jnp.where(lane < DV_B, acc_scr[0] / l_scr[0], acc_scr[1] / l_scr[1])
    o_ref[0] = o * _silu(zb_ref[0])


def _attention(qp, kp, vp, zb, tq, causal):
    B, T, _ = qp.shape
    tk_all = kp.shape[1]
    tk = tq if causal else tk_all
    return pl.pallas_call(
        functools.partial(_attn_kernel, tq, tk, causal),
        grid=(B, H_B // 2, T // tq),
        in_specs=[pl.BlockSpec((1, tq, 2 * LANES), lambda b, h, i: (b, i, h)),
                  pl.BlockSpec((1, tk_all, 2 * LANES), lambda b, h, i: (b, 0, h)),
                  pl.BlockSpec((1, tk_all, LANES), lambda b, h, i: (b, 0, h)),
                  pl.BlockSpec((1, tq, LANES), lambda b, h, i: (b, i, h))],
        out_specs=pl.BlockSpec((1, tq, LANES), lambda b, h, i: (b, i, h)),
        out_shape=jax.ShapeDtypeStruct((B, T, W_B), F32),
        scratch_shapes=[pltpu.VMEM((2, tq, LANES), F32)] * 3,
        compiler_params=_cparams(3),
        name="attn",
    )(qp, kp, vp, zb)


def _kv_past_kernel(c_ref, kr_ref, wkv_ref, pm_ref, kp_ref, vp_ref):
    kv2 = _mm(c_ref[0], wkv_ref[...])
    nq = H_B * LANES
    kp_ref[0] = (kv2[:, :nq] + _mm(kr_ref[0], pm_ref[...])).astype(BF16)
    vp_ref[0] = kv2[:, nq:].astype(BF16)


def _kv_past(ckv_past, kr_past, wts, tm):
    B, P, _ = ckv_past.shape
    krp = jnp.pad(kr_past, ((0, 0), (0, 0), (KR_LANE, LANES - KR_LANE - D_ROPE)))
    tok = lambda n: pl.BlockSpec((1, tm, n), lambda b, i: (b, i, 0))
    full = lambda a: pl.BlockSpec(a.shape, lambda b, i: (0,) * a.ndim)
    return pl.pallas_call(
        _kv_past_kernel,
        grid=(B, P // tm),
        in_specs=[tok(KV_LORA), tok(LANES), full(wts["wkv"]), full(wts["pm"])],
        out_specs=[tok(H_B * LANES), tok(W_B)],
        out_shape=[jax.ShapeDtypeStruct((B, P, H_B * LANES), BF16), jax.ShapeDtypeStruct((B, P, W_B), BF16)],
        compiler_params=_cparams(2),
        name="kv_past",
    )(ckv_past, krp, wts["wkv"], wts["pm"])


def _out_kernel(n_in, *refs):
    ins = refs[:n_in]
    ws = refs[n_in:2 * n_in]
    x_ref, mod_ref, gpost_ref, o_ref = refs[2 * n_in:]
    acc = _mm(ins[0][0], ws[0][...])
    for a, w in zip(ins[1:], ws[1:]):
        acc = acc + _mm(a[0], w[...])
    o_ref[0] = x_ref[0] + mod_ref[0, 2:3, :] * _rms(acc, gpost_ref[...])


def _out_proj(ins, ws, x, mod, g_post, tm):
    B, T, _ = x.shape
    tok = lambda n: pl.BlockSpec((1, tm, n), lambda b, i: (b, i, 0))
    full = lambda a: pl.BlockSpec(a.shape, lambda b, i: (0,) * a.ndim)
    return pl.pallas_call(
        functools.partial(_out_kernel, len(ins)),
        grid=(B, T // tm),
        in_specs=[tok(a.shape[-1]) for a in ins] + [full(w) for w in ws]
        + [tok(D), pl.BlockSpec((1, 3, D), lambda b, i: (b, 0, 0)), full(g_post)],
        out_specs=tok(D),
        out_shape=jax.ShapeDtypeStruct((B, T, D), F32),
        compiler_params=_cparams(2),
        name="out_proj",
    )(*ins, *ws, x, mod, g_post)


def _odd_layer_kernel(layer, nc, x_ref, mod_ref, gpre_ref, w_ref, lbl_ref, s0_ref, gn_ref, wout_ref, gpost_ref,
                      xo_ref, sout_ref, qs_ref, lf_ref, iv_ref, z_ref, o_ref, s_scr):
    @pl.when(pl.program_id(1) == 0)
    def _():
        s_scr[...] = s0_ref[0]

    x = x_ref[0]
    hb = _prenorm(x, mod_ref, gpre_ref).astype(BF16)
    proj = jnp.dot(hb, w_ref[...], preferred_element_type=F32)
    n = H_C * DK_C
    lg = lbl_ref[...]
    e = jnp.exp(lg - jnp.max(lg, axis=0, keepdims=True))
    p = e / jnp.sum(e, axis=0, keepdims=True)
    lb = jnp.sum(p[1:layer + 1], axis=0, keepdims=True)
    fg = lb + (1.0 - lb) * _sigmoid(proj[:, n:2 * n])
    qs_ref[...] = _silu(proj[:, :n])
    lf_ref[...] = jnp.log(fg)
    iv_ref[...] = proj[:, 2 * n:2 * n + W_C]
    z_ref[...] = proj[:, 2 * n + W_C:]

    nsub = CHUNK // SUB
    r2, c2 = _iota2((2 * CHUNK, CHUNK))
    rr = r2 & (CHUNK - 1)
    cum_l = ((c2 <= rr) & ((r2 < CHUNK) | ((c2 >> 4) == (rr >> 4)))).astype(F32).astype(BF16)
    rm, cm = _iota2((CHUNK, nsub * CHUNK))
    pmask = ((rm >> 4) == (cm >> 6)) & ((cm & (CHUNK - 1)) <= rm)

    heads = range(H_C)
    hl = [slice(h * DK_C, (h + 1) * DK_C) for h in heads]

    def chunk(ci, carry):
        rows = pl.ds(pl.multiple_of(ci * CHUNK, CHUNK), CHUNK)
        lf = lf_ref[rows, :]
        qs = qs_ref[rows, :]
        ivb = iv_ref[rows, :].astype(BF16)
        gw = _cum_left(cum_l, lf)
        g, wloc = gw[:CHUNK], gw[CHUNK:]
        k = 1.0 - jnp.exp(lf)
        segs = [(k * jnp.exp(jnp.minimum(-g, EXP_CLAMP))).astype(BF16)]
        for si in range(1, nsub):
            segs.append((k * jnp.exp(jnp.minimum(g[si * SUB - 1:si * SUB, :] - g, EXP_CLAMP))).astype(BF16))
        kstack = jnp.concatenate(segs, axis=0)
        qt = (qs * jnp.exp(wloc)).astype(BF16)
        qg = (qs * jnp.exp(g)).astype(BF16)
        gl = g[CHUNK - 1:CHUNK, :]
        kt = (k * jnp.exp(gl - g)).astype(BF16)
        dec = jnp.exp(gl)
        itile = jnp.concatenate([ivb] * nsub, axis=0)
        st = [s_scr[h] for h in heads]
        sc = [_mm_nt(qt[:, hl[h]], kstack[:, hl[h]]) for h in heads]
        upd = [_mm_tn(ivb[:, hl[h]], kt[:, hl[h]]) for h in heads]
        inter = [_mm_nt(qg[:, hl[h]], st[h]) for h in heads]
        p = [jnp.where(pmask, x, 0.0).astype(BF16) for x in sc]
        o = [_mm(p[h], itile[:, hl[h]]) + inter[h] for h in heads]
        for h in heads:
            s_scr[h] = st[h] * dec[:, hl[h]] + upd[h]
            o_ref[rows, hl[h]] = _rms(o[h], gn_ref[...]) * _silu(z_ref[rows, hl[h]])
        return carry

    lax.fori_loop(0, nc, chunk, 0, unroll=2 if nc % 2 == 0 else 1)
    sout_ref[0] = s_scr[...]
    acc = _mm(o_ref[...], wout_ref[...])
    xo_ref[0] = x + mod_ref[0, 2:3, :] * _rms(acc, gpost_ref[...])


def _odd_layer(x, mod, g_pre, g_post, w_in, w_out, lb_logits, s_hgrn, g_norm_c, layer, nc):
    B, T, _ = x.shape
    tc = nc * CHUNK
    tok = pl.BlockSpec((1, tc, D), lambda b, i: (b, i, 0))
    st = pl.BlockSpec((1, H_C, DV_C, DK_C), lambda b, i: (b, 0, 0, 0))
    full = lambda a: pl.BlockSpec(a.shape, lambda b, i: (0,) * a.ndim)
    x_new, s_new = pl.pallas_call(
        functools.partial(_odd_layer_kernel, layer, nc),
        grid=(B, T // tc),
        in_specs=[tok, pl.BlockSpec((1, 3, D), lambda b, i: (b, 0, 0)), full(g_pre), full(w_in), full(lb_logits),
                  st, full(g_norm_c), full(w_out), full(g_post)],
        out_specs=[tok, st],
        out_shape=[jax.ShapeDtypeStruct((B, T, D), F32), jax.ShapeDtypeStruct((B, H_C, DV_C, DK_C), F32)],
        scratch_shapes=[pltpu.VMEM((tc, W_C), F32)] * 5 + [pltpu.VMEM((H_C, DV_C, DK_C), F32)],
        compiler_params=_cparams(2),
        name="odd_layer",
    )(x, mod, g_pre, w_in, lb_logits, jnp.swapaxes(s_hgrn, -1, -2), g_norm_c, w_out, g_post)
    return x_new, jnp.swapaxes(s_new, -1, -2)


def _even_weights(w_in, g_q, w_uq, g_kv, w_ukv):
    o_b, o_a, o_za = QKV_A, QKV_A + H_A, QKV_A + 2 * H_A
    o_cq = o_za + W_A
    o_ckv = o_cq + Q_LORA
    o_kr = o_ckv + KV_LORA
    o_zb = o_kr + D_ROPE
    half = D_ROPE // 2
    wbig = jnp.concatenate([w_in[:, :QKV_A], w_in[:, o_za:o_cq], w_in[:, o_cq:o_ckv], w_in[:, o_ckv:o_kr],
                            w_in[:, o_zb:]], axis=1)
    kr = w_in[:, o_kr:o_zb]
    z = lambda n: jnp.zeros((D, n), F32)
    wsm = jnp.concatenate([w_in[:, o_b:o_za], kr, z(LANES - KR_LANE - D_ROPE),
                           z(KR_LANE), -kr[:, half:], kr[:, :half], z(LANES - KR_LANE - D_ROPE)], axis=1)
    w3 = w_uq.reshape(Q_LORA, H_B, D_NOPE + D_ROPE)
    nope, r1, r2 = w3[..., :D_NOPE], w3[..., D_NOPE:D_NOPE + half], w3[..., D_NOPE + half:]
    zq = lambda n: jnp.zeros((Q_LORA, H_B, n), F32)
    pad = LANES - D_NOPE - D_ROPE
    wq = jnp.concatenate([jnp.concatenate([nope, r1, r2, zq(pad)], -1).reshape(Q_LORA, H_B * LANES),
                          jnp.concatenate([zq(D_NOPE), -r2, r1, zq(pad)], -1).reshape(Q_LORA, H_B * LANES)], axis=1)
    k3 = w_ukv.reshape(KV_LORA, H_B, D_NOPE + DV_B)
    wk = jnp.concatenate([k3[..., :D_NOPE], jnp.zeros((KV_LORA, H_B, LANES - D_NOPE), F32)], -1)
    wkv = jnp.concatenate([wk.reshape(KV_LORA, H_B * LANES), k3[..., D_NOPE:].reshape(KV_LORA, W_B)], axis=1)
    j = jnp.arange(D_ROPE)
    pm = jnp.zeros((LANES, H_B, LANES), F32).at[KR_LANE + j, :, D_NOPE + j].set(1.0).reshape(LANES, H_B * LANES)
    return dict(wbig=wbig.astype(BF16), wsm=wsm.astype(BF16), wq=wq.astype(BF16), wkv=wkv.astype(BF16),
                pm=pm.astype(BF16), gq=g_q.reshape(1, Q_LORA), gkv=g_kv.reshape(1, KV_LORA))


def _rope_tables(pos):
    half = D_ROPE // 2
    inv = ROPE_THETA ** (-jnp.arange(half, dtype=F32) / half)
    ang = pos.astype(F32)[:, None] * inv[None, :]
    cos, sin = jnp.cos(ang), jnp.sin(ang)
    t = pos.shape[0]
    one, zero = jnp.ones, jnp.zeros
    scale = (D_NOPE + D_ROPE) ** -0.5
    pad_q = LANES - D_NOPE - D_ROPE
    pad_k = LANES - KR_LANE - D_ROPE
    cq = scale * jnp.concatenate([one((t, D_NOPE), F32), cos, cos, zero((t, pad_q), F32)], axis=1)
    sq = scale * jnp.concatenate([zero((t, D_NOPE), F32), sin, sin, zero((t, pad_q), F32)], axis=1)
    ck = jnp.concatenate([one((t, KR_LANE), F32), cos, cos, zero((t, pad_k), F32)], axis=1)
    sk = jnp.concatenate([zero((t, KR_LANE), F32), sin, sin, zero((t, pad_k), F32)], axis=1)
    return cq, sq, ck, sk


def _tiles(T):
    tm = min(256, T)
    nc = min(4, T // CHUNK)
    tq = min(512, T)
    return tm, nc, tq


def _trunk(x, mod, ckv_past, kr_past, s_delta, s_conv, s_hgrn, prm, tiles=None):
    B, T, _ = x.shape
    tm, nc, tq = tiles or _tiles(T)
    past = 0 if ckv_past is None else ckv_past.shape[2]
    tabs = _rope_tables(past + jnp.arange(T, dtype=jnp.int32))
    nd, ncv, nk, nr, nh = [], [], [], [], []
    for l in range(DEPTH):
        m = mod[l].reshape(B, 3, D)
        g_pre, g_post = prm["g_pre"][l].reshape(1, D), prm["g_post"][l].reshape(1, D)
        if l % 2 == 0:
            e = l // 2
            wts = prm["even"][e]
            qkv, za, zb, small, ckv, qp, kp, vp = _even_in(x, m, g_pre, wts, tabs, tm)
            o_a, s_new, conv_new = _gdn(qkv, small, za, prm["conv_w"][e], s_conv[e], s_delta[e], prm["a_log"][e],
                                        prm["dt_bias"][e], prm["g_norm_a"][e].reshape(1, DV_A), nc)
            if past:
                kpast, vpast = _kv_past(ckv_past[e], kr_past[e], wts, min(512, past))
                kp = jnp.concatenate([kpast, kp], axis=1)
                vp = jnp.concatenate([vpast, vp], axis=1)
            o_b = _attention(qp, kp, vp, zb, tq if not past else T, causal=not past)
            w_out = prm["w_out_e"][e]
            x = _out_proj([o_a, o_b], [w_out[:W_A], w_out[W_A:]], x, m, g_post, tm)
            nd.append(s_new)
            ncv.append(conv_new)
            nk.append(ckv)
            nr.append(small[..., KR_LANE:KR_LANE + D_ROPE])
        else:
            j = l // 2
            x, s_new = _odd_layer(x, m, g_pre, g_post, prm["w_in_o"][j], prm["w_out_o"][j], prm["lb_logits"],
                                  s_hgrn[j], prm["g_norm_c"][j].reshape(1, DV_C), l, nc)
            nh.append(s_new)
    return (x, jnp.stack(nd), jnp.stack(ncv), jnp.stack(nk), jnp.stack(nr), jnp.stack(nh))


def _forward(x_prompt, x_sample, c_prompt, c_sample, cache_ckv, cache_kr, state_delta, state_conv, state_hgrn,
             w_ada, b_ada, g_pre, g_post, w_in_e, conv_w, a_log, dt_bias, g_norm_a, g_q, w_uq, g_kv, w_ukv,
             w_out_e, w_in_o, lb_logits, g_norm_c, w_out_o, tiles_prompt=None, tiles_sample=None):
    bp, bs = x_prompt.shape[0], x_sample.shape[0]
    n_even, n_odd = w_in_e.shape[0], w_in_o.shape[0]
    mod = _ada(jnp.concatenate([c_prompt, c_sample], axis=0), w_ada, b_ada)
    prm = dict(
        g_pre=g_pre, g_post=g_post, conv_w=conv_w, a_log=a_log, dt_bias=dt_bias, g_norm_a=g_norm_a,
        g_norm_c=g_norm_c, lb_logits=lb_logits,
        even=[_even_weights(w_in_e[e], g_q[e], w_uq[e], g_kv[e], w_ukv[e]) for e in range(n_even)],
        w_out_e=w_out_e.astype(BF16), w_in_o=w_in_o.astype(BF16), w_out_o=w_out_o.astype(BF16))
    zeros = lambda *s: jnp.zeros(s, F32)
    outs_p = _trunk(x_prompt, mod[:, :bp], None, None, zeros(n_even, bp, H_A, DK_A, DV_A),
                    zeros(n_even, bp, CONV_W - 1, QKV_A), zeros(n_odd, bp, H_C, DK_C, DV_C), prm, tiles_prompt)
    outs_s = _trunk(x_sample, mod[:, bp:], cache_ckv, cache_kr, state_delta, state_conv, state_hgrn, prm,
                    tiles_sample)
    return (outs_p[0], outs_s[0]) + outs_p[1:] + outs_s[1:]


def kernel(x_prompt, x_sample, c_prompt, c_sample, cache_ckv, cache_kr, state_delta, state_conv, state_hgrn,
           w_ada, b_ada, g_pre, g_post, w_in_e, conv_w, a_log, dt_bias, g_norm_a, g_q, w_uq, g_kv, w_ukv,
           w_out_e, w_in_o, lb_logits, g_norm_c, w_out_o):
    return _forward(x_prompt, x_sample, c_prompt, c_sample, cache_ckv, cache_kr, state_delta, state_conv,
                    state_hgrn, w_ada, b_ada, g_pre, g_post, w_in_e, conv_w, a_log, dt_bias, g_norm_a, g_q, w_uq,
                    g_kv, w_ukv, w_out_e, w_in_o, lb_logits, g_norm_c, w_out_o)
```

```python
import functools
import math

import jax
import jax.numpy as jnp
from jax import lax
from jax.experimental import pallas as pl
from jax.experimental.pallas import tpu as pltpu

F32 = jnp.float32
BF16 = jnp.bfloat16

D = 1024
DEPTH = 4
CHUNK = 64
EPS = 1e-6
H_A, DK_A, DV_A, CONV_W = 4, 128, 128, 4
H_B, Q_LORA, KV_LORA, D_NOPE, D_ROPE, DV_B = 8, 384, 256, 64, 32, 64
ROPE_THETA = 10000.0
H_C, DK_C, DV_C = 8, 128, 128
W_A, W_B, W_C = H_A * DV_A, H_B * DV_B, H_C * DV_C
QKV_A = H_A * (2 * DK_A + DV_A)
LANES = 128
SUB = 16
KS_USED = SUB * (CHUNK // SUB) * (CHUNK // SUB + 1) // 2
KS_ROWS = 2 * LANES
EXP_CLAMP = 60.0
NEG_BIG = -1e30
VMEM_LIMIT = 56 * 1024 * 1024


def _cparams(n_axes):
    return pltpu.CompilerParams(dimension_semantics=("arbitrary",) * n_axes, vmem_limit_bytes=VMEM_LIMIT)


def _mm(a, b):
    return jnp.dot(a.astype(BF16), b.astype(BF16), preferred_element_type=F32)


def _mm_nt(a, b):
    return lax.dot_general(a.astype(BF16), b.astype(BF16), (((1,), (1,)), ((), ())), preferred_element_type=F32)


def _mm_tn(a, b):
    return lax.dot_general(a.astype(BF16), b.astype(BF16), (((0,), (0,)), ((), ())), preferred_element_type=F32)


def _split3(x):
    hi = x.astype(BF16)
    r1 = x - hi.astype(F32)
    mid = r1.astype(BF16)
    lo = (r1 - mid.astype(F32)).astype(BF16)
    return hi, mid, lo


def _cum_left(l01, x):
    hi, mid, lo = _split3(x)
    d = functools.partial(jnp.dot, preferred_element_type=F32)
    return d(l01, hi) + d(l01, mid) + d(l01, lo)


def _cum_right(x, r01):
    hi, mid, lo = _split3(x)
    d = functools.partial(jnp.dot, preferred_element_type=F32)
    return d(hi, r01) + d(mid, r01) + d(lo, r01)


def _sigmoid(x):
    return 1.0 / (1.0 + jnp.exp(-x))


def _silu(x):
    return x * _sigmoid(x)


def _softplus(x):
    return jnp.maximum(x, 0.0) + jnp.log(1.0 + jnp.exp(-jnp.abs(x)))


def _rms(x, g):
    return x * lax.rsqrt(jnp.mean(x * x, axis=-1, keepdims=True) + EPS) * g


def _prenorm(x, mod_ref, g_ref):
    return _rms(x, g_ref[...]) * (1.0 + mod_ref[0, 1:2, :]) + mod_ref[0, 0:1, :]


def _iota2(shape):
    return lax.broadcasted_iota(jnp.int32, shape, 0), lax.broadcasted_iota(jnp.int32, shape, 1)


def _ada_kernel(c_ref, w_ref, b_ref, o_ref):
    o_ref[0] = _mm(_silu(c_ref[...]), w_ref[0]) + b_ref[0]


def _ada(c, w_ada, b_ada):
    nb = c.shape[0]
    tn = 1024
    return pl.pallas_call(
        _ada_kernel,
        grid=(DEPTH, 3 * D // tn),
        in_specs=[pl.BlockSpec((nb, D), lambda l, j: (0, 0)),
                  pl.BlockSpec((1, D, tn), lambda l, j: (l, 0, j)),
                  pl.BlockSpec((1, 1, tn), lambda l, j: (l, 0, j))],
        out_specs=pl.BlockSpec((1, nb, tn), lambda l, j: (l, 0, j)),
        out_shape=jax.ShapeDtypeStruct((DEPTH, nb, 3 * D), F32),
        compiler_params=_cparams(2),
        name="ada_mod",
    )(c, w_ada, b_ada.reshape(DEPTH, 1, 3 * D))


NB_E = QKV_A + W_A + Q_LORA + KV_LORA + W_B
KR_LANE = 8


def _value_ones(n):
    lane = lax.broadcasted_iota(jnp.int32, (1, n), 1)
    return ((lane & (LANES - 1)) >= DV_B).astype(F32)


def _even_in_kernel(x_ref, mod_ref, gpre_ref, wbig_ref, wsm_ref, gq_ref, wq_ref, gkv_ref, wkv_ref, pm_ref,
                    cq_ref, sq_ref, ck_ref, sk_ref,
                    qkv_ref, za_ref, zb_ref, small_ref, ckv_ref, qp_ref, kp_ref, vp_ref):
    hb = _prenorm(x_ref[0], mod_ref, gpre_ref).astype(BF16)
    big = jnp.dot(hb, wbig_ref[...], preferred_element_type=F32)
    o = 0
    qkv_ref[0] = big[:, o:o + QKV_A]
    o += QKV_A
    za_ref[0] = big[:, o:o + W_A]
    o += W_A
    cq = big[:, o:o + Q_LORA]
    o += Q_LORA
    ckv = big[:, o:o + KV_LORA]
    o += KV_LORA
    zb_ref[0] = big[:, o:o + W_B]
    sm = jnp.dot(hb, wsm_ref[...], preferred_element_type=F32)
    small = sm[:, :LANES] * ck_ref[...] + sm[:, LANES:] * sk_ref[...]
    small_ref[0] = small
    q2 = _mm(_rms(cq, gq_ref[...]), wq_ref[...])
    nq = H_B * LANES
    cq_t = jnp.concatenate([cq_ref[...]] * H_B, axis=1)
    sq_t = jnp.concatenate([sq_ref[...]] * H_B, axis=1)
    qp_ref[0] = (q2[:, :nq] * cq_t + q2[:, nq:] * sq_t).astype(BF16)
    cn = _rms(ckv, gkv_ref[...])
    ckv_ref[0] = cn
    kv2 = _mm(cn, wkv_ref[...])
    kp_ref[0] = (kv2[:, :nq] + _mm(small, pm_ref[...])).astype(BF16)
    vp_ref[0] = (kv2[:, nq:] + _value_ones(nq)).astype(BF16)


def _even_in(x, mod, g_pre, wts, tabs, tm):
    B, T, _ = x.shape
    grid = (B, T // tm)
    full = lambda a: pl.BlockSpec(a.shape, lambda b, i: (0,) * a.ndim)
    tok = lambda n: pl.BlockSpec((1, tm, n), lambda b, i: (b, i, 0))
    tab = pl.BlockSpec((tm, LANES), lambda b, i: (i, 0))
    outs = [(QKV_A, F32), (W_A, F32), (W_B, F32), (LANES, F32), (KV_LORA, F32),
            (H_B * LANES, BF16), (H_B * LANES, BF16), (H_B * LANES, BF16)]
    return pl.pallas_call(
        _even_in_kernel,
        grid=grid,
        in_specs=[tok(D), pl.BlockSpec((1, 3, D), lambda b, i: (b, 0, 0)), full(g_pre),
                  full(wts["wbig"]), full(wts["wsm"]), full(wts["gq"]), full(wts["wq"]), full(wts["gkv"]),
                  full(wts["wkv"]), full(wts["pm"]), tab, tab, tab, tab],
        out_specs=[tok(n) for n, _ in outs],
        out_shape=[jax.ShapeDtypeStruct((B, T, n), dt) for n, dt in outs],
        compiler_params=_cparams(2),
        name="even_in",
    )(x, mod, g_pre, wts["wbig"], wts["wsm"], wts["gq"], wts["wq"], wts["gkv"], wts["wkv"], wts["pm"], *tabs)


def _unit_lower_inverse(ms, r, c):
    eye = (r == c).astype(F32)
    same16 = (r >> 4) == (c >> 4)
    same32 = (r >> 5) == (c >> 5)
    off16 = same32 & jnp.logical_not(same16)
    md = [jnp.where(same16, m, 0.0) for m in ms]
    m1 = [jnp.where(off16, m, 0.0).astype(BF16) for m in ms]
    m2 = [jnp.where(same32, 0.0, m).astype(BF16) for m in ms]
    p = [eye - x for x in md]
    q = [_mm(x, x) for x in md]
    for step in range(3):
        p = [a + _mm(a, b) for a, b in zip(p, q)]
        if step < 2:
            q = [_mm(b, b) for b in q]
    for mk in (m1, m2):
        t = [_mm(a, b) for a, b in zip(p, mk)]
        p = [a - _mm(b, a) for a, b in zip(p, t)]
    return p


def _gdn_kernel(nc, qkv_ref, sm_ref, smt_ref, za_ref, convw_ref, cbuf_ref, s0_ref, rowc_ref, colc_ref, gn_ref,
                o_ref, sout_ref, cout_ref, xpad, act, s_scr):
    tc = nc * CHUNK

    @pl.when(pl.program_id(1) == 0)
    def _():
        xpad[0:8, :] = cbuf_ref[0]
        s_scr[...] = s0_ref[0]

    xpad[8:8 + tc, :] = qkv_ref[0]
    w = convw_ref[...]
    y = (w[0:1] * xpad[5:5 + tc, :] + w[1:2] * xpad[6:6 + tc, :] + w[2:3] * xpad[7:7 + tc, :]
         + w[3:4] * xpad[8:8 + tc, :])
    act[...] = _silu(y)
    tail = xpad[tc:tc + 8, :]
    cout_ref[0] = tail
    xpad[0:8, :] = tail

    r, c = _iota2((CHUNK, CHUNK))
    tril = c <= r
    strict = c < r
    tril_b = tril.astype(F32).astype(BF16)
    triu_b = (r <= c).astype(F32).astype(BF16)
    dtb_row, alog_row = rowc_ref[0:1, :], rowc_ref[1:2, :]
    dtb_col, alog_col = colc_ref[0, :, 0:CHUNK], colc_ref[1, :, 0:CHUNK]

    items = [(ci, h) for ci in range(nc) for h in range(H_A)]
    rows = [slice(ci * CHUNK, (ci + 1) * CHUNK) for ci in range(nc)]
    sm = sm_ref[0]
    beta_all = _sigmoid(sm)
    g_all = -jnp.exp(alog_row) * _softplus(sm + dtb_row)
    gc_all = [_cum_left(tril_b, g_all[rows[ci]]) for ci in range(nc)]
    gct = [_cum_right(-jnp.exp(alog_col) * _softplus(smt_ref[0, ci] + dtb_col), triu_b) for ci in range(nc)]

    def l2n(x):
        return x * lax.rsqrt(jnp.sum(x * x, axis=-1, keepdims=True) + EPS)

    qs = [l2n(act[rows[ci], h * DK_A:(h + 1) * DK_A]) * DK_A ** -0.5 for ci, h in items]
    ks = [l2n(act[rows[ci], (H_A + h) * DK_A:(H_A + h + 1) * DK_A]) for ci, h in items]
    vs = [act[rows[ci], 2 * H_A * DK_A + h * DV_A:2 * H_A * DK_A + (h + 1) * DV_A] for ci, h in items]
    bcol = [beta_all[rows[ci], h:h + 1] for ci, h in items]
    gcol = [gc_all[ci][:, H_A + h:H_A + h + 1] for ci, h in items]
    dm = [jnp.exp(jnp.where(tril, gcol[n] - gct[ci][H_A + h:H_A + h + 1, :], NEG_BIG))
          for n, (ci, h) in enumerate(items)]
    kb = [k * b for k, b in zip(ks, bcol)]
    kbf = [k.astype(BF16) for k in ks]
    ms = [_mm_nt(a, b) * jnp.where(strict, d, 0.0) for a, b, d in zip(kb, kbf, dm)]
    tinv = _unit_lower_inverse(ms, r, c)
    egc = [jnp.exp(g) for g in gcol]
    uw = [_mm(t, jnp.concatenate([v * b, x * e], axis=1)) for t, v, b, x, e in zip(tinv, vs, bcol, kb, egc)]
    attn = [(_mm_nt(q, k) * d).astype(BF16) for q, k, d in zip(qs, kbf, dm)]
    gl = [g[CHUNK - 1:CHUNK, :] for g in gcol]
    wq = [jnp.concatenate([x[:, DV_A:], q * e], axis=0).astype(BF16) for x, q, e in zip(uw, qs, egc)]
    kgt = [jnp.transpose(k * jnp.exp(l - g)).astype(BF16) for k, l, g in zip(ks, gl, gcol)]
    egl = [jnp.exp(l) for l in gl]

    s = [s_scr[h] for h in range(H_A)]
    for ci in range(nc):
        idx = [ci * H_A + h for h in range(H_A)]
        ws = [_mm(wq[n], s[h]) for h, n in enumerate(idx)]
        v_new = [uw[n][:, :DV_A] - ws[h][:CHUNK] for h, n in enumerate(idx)]
        o = [ws[h][CHUNK:] + _mm(attn[n], v_new[h]) for h, n in enumerate(idx)]
        s = [s[h] * egl[n] + _mm(kgt[n], v_new[h]) for h, n in enumerate(idx)]
        for h in range(H_A):
            lanes = slice(h * DV_A, (h + 1) * DV_A)
            o_ref[0, rows[ci], lanes] = _rms(o[h], gn_ref[...]) * _silu(za_ref[0, rows[ci], lanes])
    for h in range(H_A):
        s_scr[h] = s[h]
    sout_ref[0] = s_scr[...]


def _gdn(qkv, small, za, conv_w, conv_buf, s_delta, a_log, dt_bias, g_norm_a, nc):
    B, T, _ = qkv.shape
    tc = nc * CHUNK
    smt = small[..., :8].reshape(B, T // CHUNK, CHUNK, 8).transpose(0, 1, 3, 2)
    lane = jnp.zeros((LANES,), F32)
    rowc = jnp.stack([lane.at[H_A:2 * H_A].set(dt_bias), lane.at[H_A:2 * H_A].set(a_log)])
    col = jnp.zeros((8,), F32)
    colc = jnp.stack([col.at[H_A:2 * H_A].set(dt_bias), col.at[H_A:2 * H_A].set(a_log)])
    colc = jnp.broadcast_to(colc[:, :, None], (2, 8, LANES))
    cbuf = jnp.pad(conv_buf, ((0, 0), (8 - (CONV_W - 1), 0), (0, 0)))
    tok = lambda n: pl.BlockSpec((1, tc, n), lambda b, i: (b, i, 0))
    full = lambda a: pl.BlockSpec(a.shape, lambda b, i: (0,) * a.ndim)
    o, s_new, ctail = pl.pallas_call(
        functools.partial(_gdn_kernel, nc),
        grid=(B, T // tc),
        in_specs=[tok(QKV_A), tok(LANES), pl.BlockSpec((1, nc, 8, CHUNK), lambda b, i: (b, i, 0, 0)), tok(W_A),
                  full(conv_w), pl.BlockSpec((1, 8, QKV_A), lambda b, i: (b, 0, 0)),
                  pl.BlockSpec((1, H_A, DK_A, DV_A), lambda b, i: (b, 0, 0, 0)),
                  full(rowc), full(colc), full(g_norm_a)],
        out_specs=[tok(W_A), pl.BlockSpec((1, H_A, DK_A, DV_A), lambda b, i: (b, 0, 0, 0)),
                   pl.BlockSpec((1, 8, QKV_A), lambda b, i: (b, 0, 0))],
        out_shape=[jax.ShapeDtypeStruct((B, T, W_A), F32), jax.ShapeDtypeStruct((B, H_A, DK_A, DV_A), F32),
                   jax.ShapeDtypeStruct((B, 8, QKV_A), F32)],
        scratch_shapes=[pltpu.VMEM((tc + 8, QKV_A), F32), pltpu.VMEM((tc, QKV_A), F32),
                        pltpu.VMEM((H_A, DK_A, DV_A), F32)],
        compiler_params=_cparams(2),
        name="gdn",
    )(qkv, small, smt, za, conv_w, cbuf, s_delta, rowc, colc, g_norm_a)
    return o, s_new, ctail[:, 8 - (CONV_W - 1):, :]


def _attn_kernel(tq, causal, n_kv, q_ref, *refs):
    kv_refs = [(refs[2 * n], refs[2 * n + 1]) for n in range(n_kv)]
    zb_ref, o_ref, m_scr, acc_scr = refs[2 * n_kv:]
    i = pl.program_id(2)
    heads = (0, 1)
    hl = [slice(hh * LANES, (hh + 1) * LANES) for hh in heads]
    if causal:
        r, c = _iota2((tq, tq))
        mask = (c >> 6) <= (r >> 6)
    qs = [q_ref[0, :, hl[hh]] for hh in heads]
    m_scr[...] = jnp.full((2, tq, LANES), NEG_BIG, F32)
    acc_scr[...] = jnp.zeros((2, tq, LANES), F32)

    def tile(k_ref, v_ref, rows, tk, masked):
        s = [_mm_nt(qs[hh], k_ref[0, rows, hl[hh]]) for hh in heads]
        if masked:
            s = [jnp.where(mask, x, NEG_BIG) for x in s]
        m_prev = [m_scr[hh] for hh in heads]
        m_new = [jnp.maximum(m_prev[hh], jnp.max(s[hh], axis=-1, keepdims=True)) for hh in heads]
        alpha = [jnp.exp2(m_prev[hh] - m_new[hh]) for hh in heads]
        if tk % LANES == 0:
            p = [jnp.exp2(s[hh] - jnp.concatenate([m_new[hh]] * (tk // LANES), axis=1)) for hh in heads]
        else:
            p = [jnp.exp2(s[hh] - m_new[hh][:, 0:1]) for hh in heads]
        pv = [_mm(p[hh], v_ref[0, rows, hl[hh]]) for hh in heads]
        for hh in heads:
            acc_scr[hh] = alpha[hh] * acc_scr[hh] + pv[hh]
            m_scr[hh] = m_new[hh]

    if causal:
        k_ref, v_ref = kv_refs[0]

        def body(j, carry):
            tile(k_ref, v_ref, pl.ds(pl.multiple_of(j * tq, tq), tq), tq, False)
            return carry
        lax.fori_loop(0, i, body, 0)
        tile(k_ref, v_ref, pl.ds(pl.multiple_of(i * tq, tq), tq), tq, True)
    else:
        for k_ref, v_ref in kv_refs:
            tile(k_ref, v_ref, slice(None), k_ref.shape[1], False)
    _, lane = _iota2((tq, LANES))
    a0, a1 = acc_scr[0], acc_scr[1]
    o0 = a0 / pltpu.roll(a0, DV_B, 1)
    o1 = a1 / pltpu.roll(a1, DV_B, 1)
    o = jnp.where(lane < DV_B, o0, pltpu.roll(o1, DV_B, 1))
    o_ref[0] = o * _silu(zb_ref[0])


def _attention(qp, kvs, zb, tq, causal):
    B, T, _ = qp.shape
    pair = lambda tk: pl.BlockSpec((1, tk, 2 * LANES), lambda b, h, i: (b, 0, h))
    kv_specs, kv_args = [], []
    for k, v in kvs:
        kv_specs += [pair(k.shape[1]), pair(v.shape[1])]
        kv_args += [k, v]
    return pl.pallas_call(
        functools.partial(_attn_kernel, tq, causal, len(kvs)),
        grid=(B, H_B // 2, T // tq),
        in_specs=[pl.BlockSpec((1, tq, 2 * LANES), lambda b, h, i: (b, i, h))] + kv_specs
        + [pl.BlockSpec((1, tq, LANES), lambda b, h, i: (b, i, h))],
        out_specs=pl.BlockSpec((1, tq, LANES), lambda b, h, i: (b, i, h)),
        out_shape=jax.ShapeDtypeStruct((B, T, W_B), F32),
        scratch_shapes=[pltpu.VMEM((2, tq, LANES), F32)] * 2,
        compiler_params=_cparams(3),
        name="attn",
    )(qp, *kv_args, zb)


def _kv_past_kernel(c_ref, kr_ref, wkv_ref, pm_ref, kp_ref, vp_ref):
    kv2 = _mm(c_ref[0], wkv_ref[...])
    nq = H_B * LANES
    kp_ref[0] = (kv2[:, :nq] + _mm(kr_ref[0], pm_ref[...])).astype(BF16)
    vp_ref[0] = (kv2[:, nq:] + _value_ones(nq)).astype(BF16)


def _kv_past(ckv_past, kr_past, wts, tm):
    B, P, _ = ckv_past.shape
    krp = jnp.pad(kr_past, ((0, 0), (0, 0), (KR_LANE, LANES - KR_LANE - D_ROPE)))
    tok = lambda n: pl.BlockSpec((1, tm, n), lambda b, i: (b, i, 0))
    full = lambda a: pl.BlockSpec(a.shape, lambda b, i: (0,) * a.ndim)
    return pl.pallas_call(
        _kv_past_kernel,
        grid=(B, P // tm),
        in_specs=[tok(KV_LORA), tok(LANES), full(wts["wkv"]), full(wts["pm"])],
        out_specs=[tok(H_B * LANES), tok(H_B * LANES)],
        out_shape=[jax.ShapeDtypeStruct((B, P, H_B * LANES), BF16)] * 2,
        compiler_params=_cparams(2),
        name="kv_past",
    )(ckv_past, krp, wts["wkv"], wts["pm"])


def _out_kernel(n_in, *refs):
    ins = refs[:n_in]
    ws = refs[n_in:2 * n_in]
    x_ref, mod_ref, gpost_ref, o_ref = refs[2 * n_in:]
    acc = _mm(ins[0][0], ws[0][...])
    for a, w in zip(ins[1:], ws[1:]):
        acc = acc + _mm(a[0], w[...])
    o_ref[0] = x_ref[0] + mod_ref[0, 2:3, :] * _rms(acc, gpost_ref[...])


def _out_proj(ins, ws, x, mod, g_post, tm):
    B, T, _ = x.shape
    tok = lambda n: pl.BlockSpec((1, tm, n), lambda b, i: (b, i, 0))
    full = lambda a: pl.BlockSpec(a.shape, lambda b, i: (0,) * a.ndim)
    return pl.pallas_call(
        functools.partial(_out_kernel, len(ins)),
        grid=(B, T // tm),
        in_specs=[tok(a.shape[-1]) for a in ins] + [full(w) for w in ws]
        + [tok(D), pl.BlockSpec((1, 3, D), lambda b, i: (b, 0, 0)), full(g_post)],
        out_specs=tok(D),
        out_shape=jax.ShapeDtypeStruct((B, T, D), F32),
        compiler_params=_cparams(2),
        name="out_proj",
    )(*ins, *ws, x, mod, g_post)


def _odd_layer_kernel(layer, nc, x_ref, mod_ref, gpre_ref, w_ref, lbl_ref, s0_ref, gn_ref, wout_ref, gpost_ref,
                      xo_ref, sout_ref, qs_ref, lf_ref, iv_ref, z_ref, o_ref, s_scr, *bufs):
    @pl.when(pl.program_id(1) == 0)
    def _():
        s_scr[...] = s0_ref[0]

    x = x_ref[0]
    hb = _prenorm(x, mod_ref, gpre_ref).astype(BF16)
    proj = jnp.dot(hb, w_ref[...], preferred_element_type=F32)
    n = H_C * DK_C
    lg = lbl_ref[...]
    e = jnp.exp(lg - jnp.max(lg, axis=0, keepdims=True))
    p = e / jnp.sum(e, axis=0, keepdims=True)
    lb = jnp.sum(p[1:layer + 1], axis=0, keepdims=True)
    fg = lb + (1.0 - lb) * _sigmoid(proj[:, n:2 * n])
    qs_ref[...] = _silu(proj[:, :n])
    lf_ref[...] = jnp.log(fg)
    iv_ref[...] = proj[:, 2 * n:2 * n + W_C]
    z_ref[...] = proj[:, 2 * n + W_C:]

    nsub = CHUNK // SUB
    r2, c2 = _iota2((CHUNK, CHUNK))
    cum_l = (c2 <= r2).astype(F32).astype(BF16)
    rm, cm = _iota2((CHUNK, KS_ROWS))
    seg = jnp.zeros_like(cm)
    off = jnp.zeros_like(cm)
    for si in range(1, nsub):
        start = SUB * si * (si + 1) // 2
        seg = seg + (cm >= start).astype(jnp.int32)
        off = off + jnp.where(cm >= start, SUB * si, 0)
    pmask = ((rm >> 4) == seg) & ((cm - off) <= rm) & (cm < KS_USED)

    heads = range(H_C)
    hl = [slice(h * DK_C, (h + 1) * DK_C) for h in heads]

    def prep(ci, buf):
        qt_s, qg_s, kt_s, ks_s, it_s, dec_s = buf
        rows = pl.ds(pl.multiple_of(ci * CHUNK, CHUNK), CHUNK)
        lf = lf_ref[rows, :]
        qs = qs_ref[rows, :]
        ivb = iv_ref[rows, :].astype(BF16)
        g = _cum_left(cum_l, lf)
        wloc = jnp.concatenate([g[:SUB]] + [g[j * SUB:(j + 1) * SUB] - g[j * SUB - 1:j * SUB, :]
                                            for j in range(1, CHUNK // SUB)], axis=0)
        k = 1.0 - jnp.exp(lf)
        sub = [slice(j * SUB, (j + 1) * SUB) for j in range(nsub)]
        bound = [g[(j + 1) * SUB - 1:(j + 1) * SUB, :] for j in range(nsub)]
        diag = (k * jnp.exp(jnp.minimum(-wloc, EXP_CLAMP))).astype(BF16)
        base = [k[sub[j]] * jnp.exp(wloc[(j + 1) * SUB - 1:(j + 1) * SUB, :] - wloc[sub[j]]) for j in range(nsub - 1)]
        pieces = []
        for si in range(nsub):
            for j in range(si):
                far = base[j] if j + 1 == si else base[j] * jnp.exp(bound[si - 1] - bound[j])
                pieces.append(far.astype(BF16))
            pieces.append(diag[sub[si]])
        pieces.append(jnp.zeros((KS_ROWS - KS_USED, W_C), BF16))
        ks_s[...] = jnp.concatenate(pieces, axis=0)
        qt_s[...] = (qs * jnp.exp(wloc)).astype(BF16)
        qg_s[...] = (qs * jnp.exp(g)).astype(BF16)
        gl = g[CHUNK - 1:CHUNK, :]
        kt_s[...] = (k * jnp.exp(gl - g)).astype(BF16)
        dec_s[...] = jnp.broadcast_to(jnp.exp(gl), dec_s.shape)
        it_s[...] = jnp.concatenate([ivb[:(si + 1) * SUB] for si in range(nsub)]
                                    + [jnp.zeros((KS_ROWS - KS_USED, W_C), BF16)], axis=0)

    def mat(ci, buf):
        qt_s, qg_s, kt_s, ks_s, it_s, dec_s = buf
        rows = pl.ds(pl.multiple_of(ci * CHUNK, CHUNK), CHUNK)
        last = slice(KS_USED - CHUNK, KS_USED)
        st = [s_scr[h] for h in heads]
        sc = [_mm_nt(qt_s[:, hl[h]], ks_s[:, hl[h]]) for h in heads]
        upd = [_mm_tn(it_s[last, hl[h]], kt_s[:, hl[h]]) for h in heads]
        inter = [_mm_nt(qg_s[:, hl[h]], st[h]) for h in heads]
        p = [jnp.where(pmask, x, 0.0).astype(BF16) for x in sc]
        o = [_mm(p[h], it_s[:, hl[h]]) + inter[h] for h in heads]
        for h in heads:
            s_scr[h] = st[h] * dec_s[0:1, hl[h]] + upd[h]
            o_ref[rows, hl[h]] = _rms(o[h], gn_ref[...]) * _silu(z_ref[rows, hl[h]])

    buf_a, buf_b = bufs[:6], bufs[6:]
    prep(0, buf_a)
    if nc == 1:
        mat(0, buf_a)
    else:
        def pair(i, carry):
            c0 = 2 * i
            prep(c0 + 1, buf_b)
            mat(c0, buf_a)
            prep(jnp.minimum(c0 + 2, nc - 1), buf_a)
            mat(c0 + 1, buf_b)
            return carry
        lax.fori_loop(0, nc // 2, pair, 0)
    sout_ref[0] = s_scr[...]
    acc = _mm(o_ref[...], wout_ref[...])
    xo_ref[0] = x + mod_ref[0, 2:3, :] * _rms(acc, gpost_ref[...])


def _odd_layer(x, mod, g_pre, g_post, w_in, w_out, lb_logits, s_hgrn, g_norm_c, layer, nc):
    B, T, _ = x.shape
    assert nc == 1 or nc % 2 == 0
    tc = nc * CHUNK
    operand_bufs = [pltpu.VMEM((CHUNK, W_C), BF16)] * 3 + [pltpu.VMEM((KS_ROWS, W_C), BF16)] * 2 + [
        pltpu.VMEM((8, W_C), F32)]
    tok = pl.BlockSpec((1, tc, D), lambda b, i: (b, i, 0))
    st = pl.BlockSpec((1, H_C, DV_C, DK_C), lambda b, i: (b, 0, 0, 0))
    full = lambda a: pl.BlockSpec(a.shape, lambda b, i: (0,) * a.ndim)
    x_new, s_new = pl.pallas_call(
        functools.partial(_odd_layer_kernel, layer, nc),
        grid=(B, T // tc),
        in_specs=[tok, pl.BlockSpec((1, 3, D), lambda b, i: (b, 0, 0)), full(g_pre), full(w_in), full(lb_logits),
                  st, full(g_norm_c), full(w_out), full(g_post)],
        out_specs=[tok, st],
        out_shape=[jax.ShapeDtypeStruct((B, T, D), F32), jax.ShapeDtypeStruct((B, H_C, DV_C, DK_C), F32)],
        scratch_shapes=[pltpu.VMEM((tc, W_C), F32)] * 5 + [pltpu.VMEM((H_C, DV_C, DK_C), F32)] + operand_bufs * 2,
        compiler_params=_cparams(2),
        name="odd_layer",
    )(x, mod, g_pre, w_in, lb_logits, jnp.swapaxes(s_hgrn, -1, -2), g_norm_c, w_out, g_post)
    return x_new, jnp.swapaxes(s_new, -1, -2)


def _even_weights(w_in, g_q, w_uq, g_kv, w_ukv):
    o_b, o_a, o_za = QKV_A, QKV_A + H_A, QKV_A + 2 * H_A
    o_cq = o_za + W_A
    o_ckv = o_cq + Q_LORA
    o_kr = o_ckv + KV_LORA
    o_zb = o_kr + D_ROPE
    half = D_ROPE // 2
    wbig = jnp.concatenate([w_in[:, :QKV_A], w_in[:, o_za:o_cq], w_in[:, o_cq:o_ckv], w_in[:, o_ckv:o_kr],
                            w_in[:, o_zb:]], axis=1)
    kr = w_in[:, o_kr:o_zb]
    z = lambda n: jnp.zeros((D, n), F32)
    wsm = jnp.concatenate([w_in[:, o_b:o_za], kr, z(LANES - KR_LANE - D_ROPE),
                           z(KR_LANE), -kr[:, half:], kr[:, :half], z(LANES - KR_LANE - D_ROPE)], axis=1)
    w3 = w_uq.reshape(Q_LORA, H_B, D_NOPE + D_ROPE)
    nope, r1, r2 = w3[..., :D_NOPE], w3[..., D_NOPE:D_NOPE + half], w3[..., D_NOPE + half:]
    zq = lambda n: jnp.zeros((Q_LORA, H_B, n), F32)
    pad = LANES - D_NOPE - D_ROPE
    wq = jnp.concatenate([jnp.concatenate([nope, r1, r2, zq(pad)], -1).reshape(Q_LORA, H_B * LANES),
                          jnp.concatenate([zq(D_NOPE), -r2, r1, zq(pad)], -1).reshape(Q_LORA, H_B * LANES)], axis=1)
    k3 = w_ukv.reshape(KV_LORA, H_B, D_NOPE + DV_B)
    wk = jnp.concatenate([k3[..., :D_NOPE], jnp.zeros((KV_LORA, H_B, LANES - D_NOPE), F32)], -1)
    wv = jnp.concatenate([k3[..., D_NOPE:], jnp.zeros((KV_LORA, H_B, LANES - DV_B), F32)], -1)
    wkv = jnp.concatenate([wk.reshape(KV_LORA, H_B * LANES), wv.reshape(KV_LORA, H_B * LANES)], axis=1)
    j = jnp.arange(D_ROPE)
    pm = jnp.zeros((LANES, H_B, LANES), F32).at[KR_LANE + j, :, D_NOPE + j].set(1.0).reshape(LANES, H_B * LANES)
    return dict(wbig=wbig.astype(BF16), wsm=wsm.astype(BF16), wq=wq.astype(BF16), wkv=wkv.astype(BF16),
                pm=pm.astype(BF16), gq=g_q.reshape(1, Q_LORA), gkv=g_kv.reshape(1, KV_LORA))


def _rope_tables(pos):
    half = D_ROPE // 2
    inv = ROPE_THETA ** (-jnp.arange(half, dtype=F32) / half)
    ang = pos.astype(F32)[:, None] * inv[None, :]
    cos, sin = jnp.cos(ang), jnp.sin(ang)
    t = pos.shape[0]
    one, zero = jnp.ones, jnp.zeros
    scale = (D_NOPE + D_ROPE) ** -0.5 * math.log2(math.e)
    pad_q = LANES - D_NOPE - D_ROPE
    pad_k = LANES - KR_LANE - D_ROPE
    cq = scale * jnp.concatenate([one((t, D_NOPE), F32), cos, cos, zero((t, pad_q), F32)], axis=1)
    sq = scale * jnp.concatenate([zero((t, D_NOPE), F32), sin, sin, zero((t, pad_q), F32)], axis=1)
    ck = jnp.concatenate([one((t, KR_LANE), F32), cos, cos, zero((t, pad_k), F32)], axis=1)
    sk = jnp.concatenate([zero((t, KR_LANE), F32), sin, sin, zero((t, pad_k), F32)], axis=1)
    return cq, sq, ck, sk


def _tiles(T):
    tm = min(256, T)
    nc = min(4, T // CHUNK)
    tq = min(512, T)
    return tm, nc, tq


def _trunk(x, mod, ckv_past, kr_past, s_delta, s_conv, s_hgrn, prm, tiles=None):
    B, T, _ = x.shape
    tm, nc, tq = tiles or _tiles(T)
    past = 0 if ckv_past is None else ckv_past.shape[2]
    tabs = _rope_tables(past + jnp.arange(T, dtype=jnp.int32))
    nd, ncv, nk, nr, nh = [], [], [], [], []
    for l in range(DEPTH):
        m = mod[l].reshape(B, 3, D)
        g_pre, g_post = prm["g_pre"][l].reshape(1, D), prm["g_post"][l].reshape(1, D)
        if l % 2 == 0:
            e = l // 2
            wts = prm["even"][e]
            qkv, za, zb, small, ckv, qp, kp, vp = _even_in(x, m, g_pre, wts, tabs, tm)
            o_a, s_new, conv_new = _gdn(qkv, small, za, prm["conv_w"][e], s_conv[e], s_delta[e], prm["a_log"][e],
                                        prm["dt_bias"][e], prm["g_norm_a"][e].reshape(1, DV_A), nc)
            kvs = [(kp, vp)]
            if past:
                kvs.insert(0, _kv_past(ckv_past[e], kr_past[e], wts, min(512, past)))
            o_b = _attention(qp, kvs, zb, tq if not past else T, causal=not past)
            w_out = prm["w_out_e"][e]
            x = _out_proj([o_a, o_b], [w_out[:W_A], w_out[W_A:]], x, m, g_post, tm)
            nd.append(s_new)
            ncv.append(conv_new)
            nk.append(ckv)
            nr.append(small[..., KR_LANE:KR_LANE + D_ROPE])
        else:
            j = l // 2
            x, s_new = _odd_layer(x, m, g_pre, g_post, prm["w_in_o"][j], prm["w_out_o"][j], prm["lb_logits"],
                                  s_hgrn[j], prm["g_norm_c"][j].reshape(1, DV_C), l, min(8, T // CHUNK))
            nh.append(s_new)
    return (x, jnp.stack(nd), jnp.stack(ncv), jnp.stack(nk), jnp.stack(nr), jnp.stack(nh))


def _forward(x_prompt, x_sample, c_prompt, c_sample, cache_ckv, cache_kr, state_delta, state_conv, state_hgrn,
             w_ada, b_ada, g_pre, g_post, w_in_e, conv_w, a_log, dt_bias, g_norm_a, g_q, w_uq, g_kv, w_ukv,
             w_out_e, w_in_o, lb_logits, g_norm_c, w_out_o, tiles_prompt=None, tiles_sample=None):
    bp, bs = x_prompt.shape[0], x_sample.shape[0]
    n_even, n_odd = w_in_e.shape[0], w_in_o.shape[0]
    mod = _ada(jnp.concatenate([c_prompt, c_sample], axis=0), w_ada, b_ada)
    prm = dict(
        g_pre=g_pre, g_post=g_post, conv_w=conv_w, a_log=a_log, dt_bias=dt_bias, g_norm_a=g_norm_a,
        g_norm_c=g_norm_c, lb_logits=lb_logits,
        even=[_even_weights(w_in_e[e], g_q[e], w_uq[e], g_kv[e], w_ukv[e]) for e in range(n_even)],
        w_out_e=w_out_e.astype(BF16), w_in_o=w_in_o.astype(BF16), w_out_o=w_out_o.astype(BF16))
    zeros = lambda *s: jnp.zeros(s, F32)
    outs_p = _trunk(x_prompt, mod[:, :bp], None, None, zeros(n_even, bp, H_A, DK_A, DV_A),
                    zeros(n_even, bp, CONV_W - 1, QKV_A), zeros(n_odd, bp, H_C, DK_C, DV_C), prm, tiles_prompt)
    outs_s = _trunk(x_sample, mod[:, bp:], cache_ckv, cache_kr, state_delta, state_conv, state_hgrn, prm,
                    tiles_sample)
    return (outs_p[0], outs_s[0]) + outs_p[1:] + outs_s[1:]


def kernel(x_prompt, x_sample, c_prompt, c_sample, cache_ckv, cache_kr, state_delta, state_conv, state_hgrn,
           w_ada, b_ada, g_pre, g_post, w_in_e, conv_w, a_log, dt_bias, g_norm_a, g_q, w_uq, g_kv, w_ukv,
           w_out_e, w_in_o, lb_logits, g_norm_c, w_out_o):
    return _forward(x_prompt, x_sample, c_prompt, c_sample, cache_ckv, cache_kr, state_delta, state_conv,
                    state_hgrn, w_ada, b_ada, g_pre, g_post, w_in_e, conv_w, a_log, dt_bias, g_norm_a, g_q, w_uq,
                    g_kv, w_ukv, w_out_e, w_in_o, lb_logits, g_norm_c, w_out_o)
```

```python
import functools
import math

import jax
import jax.numpy as jnp
from jax import lax
from jax.experimental import pallas as pl
from jax.experimental.pallas import tpu as pltpu

F32 = jnp.float32
BF16 = jnp.bfloat16

D = 1024
DEPTH = 4
CHUNK = 64
EPS = 1e-6
H_A, DK_A, DV_A, CONV_W = 4, 128, 128, 4
H_B, Q_LORA, KV_LORA, D_NOPE, D_ROPE, DV_B = 8, 384, 256, 64, 32, 64
ROPE_THETA = 10000.0
H_C, DK_C, DV_C = 8, 128, 128
W_A, W_B, W_C = H_A * DV_A, H_B * DV_B, H_C * DV_C
QKV_A = H_A * (2 * DK_A + DV_A)
LANES = 128
SUB = 16
KS_USED = SUB * (CHUNK // SUB) * (CHUNK // SUB + 1) // 2
KS_ROWS = 2 * LANES
EXP_CLAMP = 60.0
NEG_BIG = -1e30
VMEM_LIMIT = 56 * 1024 * 1024


def _cparams(n_axes):
    return pltpu.CompilerParams(dimension_semantics=("arbitrary",) * n_axes, vmem_limit_bytes=VMEM_LIMIT)


def _mm(a, b):
    return jnp.dot(a.astype(BF16), b.astype(BF16), preferred_element_type=F32)


def _mm_nt(a, b):
    return lax.dot_general(a.astype(BF16), b.astype(BF16), (((1,), (1,)), ((), ())), preferred_element_type=F32)


def _mm_tn(a, b):
    return lax.dot_general(a.astype(BF16), b.astype(BF16), (((0,), (0,)), ((), ())), preferred_element_type=F32)


def _split3(x):
    hi = x.astype(BF16)
    r1 = x - hi.astype(F32)
    mid = r1.astype(BF16)
    lo = (r1 - mid.astype(F32)).astype(BF16)
    return hi, mid, lo


def _cum_left(l01, x):
    hi, mid, lo = _split3(x)
    d = functools.partial(jnp.dot, preferred_element_type=F32)
    return d(l01, hi) + d(l01, mid) + d(l01, lo)


def _cum_right(x, r01):
    hi, mid, lo = _split3(x)
    d = functools.partial(jnp.dot, preferred_element_type=F32)
    return d(hi, r01) + d(mid, r01) + d(lo, r01)


def _sigmoid(x):
    return 1.0 / (1.0 + jnp.exp(-x))


def _silu(x):
    return x * _sigmoid(x)


def _softplus(x):
    return jnp.maximum(x, 0.0) + jnp.log(1.0 + jnp.exp(-jnp.abs(x)))


def _rms(x, g):
    return x * lax.rsqrt(jnp.mean(x * x, axis=-1, keepdims=True) + EPS) * g


def _prenorm(x, mod_ref, g_ref):
    return _rms(x, g_ref[...]) * (1.0 + mod_ref[0, 1:2, :]) + mod_ref[0, 0:1, :]


def _iota2(shape):
    return lax.broadcasted_iota(jnp.int32, shape, 0), lax.broadcasted_iota(jnp.int32, shape, 1)


def _ada_kernel(c_ref, w_ref, b_ref, o_ref):
    o_ref[0] = _mm(_silu(c_ref[...]), w_ref[0]) + b_ref[0]


def _ada(c, w_ada, b_ada):
    nb = c.shape[0]
    tn = 1024
    return pl.pallas_call(
        _ada_kernel,
        grid=(DEPTH, 3 * D // tn),
        in_specs=[pl.BlockSpec((nb, D), lambda l, j: (0, 0)),
                  pl.BlockSpec((1, D, tn), lambda l, j: (l, 0, j)),
                  pl.BlockSpec((1, 1, tn), lambda l, j: (l, 0, j))],
        out_specs=pl.BlockSpec((1, nb, tn), lambda l, j: (l, 0, j)),
        out_shape=jax.ShapeDtypeStruct((DEPTH, nb, 3 * D), F32),
        compiler_params=_cparams(2),
        name="ada_mod",
    )(c, w_ada, b_ada.reshape(DEPTH, 1, 3 * D))


NB_E = QKV_A + W_A + Q_LORA + KV_LORA + W_B
KR_LANE = 8


def _value_ones(n):
    lane = lax.broadcasted_iota(jnp.int32, (1, n), 1)
    return ((lane & (LANES - 1)) >= DV_B).astype(F32)


def _even_in_kernel(x_ref, mod_ref, gpre_ref, wbig_ref, wsm_ref, gq_ref, wq_ref, gkv_ref, wkv_ref, pm_ref,
                    cq_ref, sq_ref, ck_ref, sk_ref,
                    qkv_ref, za_ref, zb_ref, small_ref, ckv_ref, qp_ref, kp_ref, vp_ref):
    hb = _prenorm(x_ref[0], mod_ref, gpre_ref).astype(BF16)
    big = jnp.dot(hb, wbig_ref[...], preferred_element_type=F32)
    o = 0
    qkv_ref[0] = big[:, o:o + QKV_A]
    o += QKV_A
    za_ref[0] = big[:, o:o + W_A]
    o += W_A
    cq = big[:, o:o + Q_LORA]
    o += Q_LORA
    ckv = big[:, o:o + KV_LORA]
    o += KV_LORA
    zb_ref[0] = big[:, o:o + W_B]
    sm = jnp.dot(hb, wsm_ref[...], preferred_element_type=F32)
    small = sm[:, :LANES] * ck_ref[...] + sm[:, LANES:] * sk_ref[...]
    small_ref[0] = small
    q2 = _mm(_rms(cq, gq_ref[...]), wq_ref[...])
    nq = H_B * LANES
    cq_t = jnp.concatenate([cq_ref[...]] * H_B, axis=1)
    sq_t = jnp.concatenate([sq_ref[...]] * H_B, axis=1)
    qp_ref[0] = (q2[:, :nq] * cq_t + q2[:, nq:] * sq_t).astype(BF16)
    cn = _rms(ckv, gkv_ref[...])
    ckv_ref[0] = cn
    kv2 = _mm(cn, wkv_ref[...])
    kp_ref[0] = (kv2[:, :nq] + _mm(small, pm_ref[...])).astype(BF16)
    vp_ref[0] = (kv2[:, nq:] + _value_ones(nq)).astype(BF16)


def _even_in(x, mod, g_pre, wts, tabs, tm):
    B, T, _ = x.shape
    grid = (B, T // tm)
    full = lambda a: pl.BlockSpec(a.shape, lambda b, i: (0,) * a.ndim)
    tok = lambda n: pl.BlockSpec((1, tm, n), lambda b, i: (b, i, 0))
    tab = pl.BlockSpec((tm, LANES), lambda b, i: (i, 0))
    outs = [(QKV_A, F32), (W_A, F32), (W_B, F32), (LANES, F32), (KV_LORA, F32),
            (H_B * LANES, BF16), (H_B * LANES, BF16), (H_B * LANES, BF16)]
    return pl.pallas_call(
        _even_in_kernel,
        grid=grid,
        in_specs=[tok(D), pl.BlockSpec((1, 3, D), lambda b, i: (b, 0, 0)), full(g_pre),
                  full(wts["wbig"]), full(wts["wsm"]), full(wts["gq"]), full(wts["wq"]), full(wts["gkv"]),
                  full(wts["wkv"]), full(wts["pm"]), tab, tab, tab, tab],
        out_specs=[tok(n) for n, _ in outs],
        out_shape=[jax.ShapeDtypeStruct((B, T, n), dt) for n, dt in outs],
        compiler_params=_cparams(2),
        name="even_in",
    )(x, mod, g_pre, wts["wbig"], wts["wsm"], wts["gq"], wts["wq"], wts["gkv"], wts["wkv"], wts["pm"], *tabs)


def _unit_lower_inverse(ms, r, c):
    eye = (r == c).astype(F32)
    same16 = (r >> 4) == (c >> 4)
    same32 = (r >> 5) == (c >> 5)
    off16 = same32 & jnp.logical_not(same16)
    md = [jnp.where(same16, m, 0.0) for m in ms]
    m1 = [jnp.where(off16, m, 0.0).astype(BF16) for m in ms]
    m2 = [jnp.where(same32, 0.0, m).astype(BF16) for m in ms]
    p = [eye - x for x in md]
    q = [_mm(x, x) for x in md]
    for step in range(3):
        p = [a + _mm(a, b) for a, b in zip(p, q)]
        if step < 2:
            q = [_mm(b, b) for b in q]
    for mk in (m1, m2):
        t = [_mm(a, b) for a, b in zip(p, mk)]
        p = [a - _mm(b, a) for a, b in zip(p, t)]
    return p


def _gdn_kernel(nc, qkv_ref, sm_ref, smt_ref, za_ref, convw_ref, cbuf_ref, s0_ref, rowc_ref, colc_ref, gn_ref,
                o_ref, sout_ref, cout_ref, xpad, act, s_scr):
    tc = nc * CHUNK

    @pl.when(pl.program_id(1) == 0)
    def _():
        xpad[0:8, :] = cbuf_ref[0]
        s_scr[...] = s0_ref[0]

    xpad[8:8 + tc, :] = qkv_ref[0]
    w = convw_ref[...]
    y = (w[0:1] * xpad[5:5 + tc, :] + w[1:2] * xpad[6:6 + tc, :] + w[2:3] * xpad[7:7 + tc, :]
         + w[3:4] * xpad[8:8 + tc, :])
    act[...] = _silu(y)
    tail = xpad[tc:tc + 8, :]
    cout_ref[0] = tail
    xpad[0:8, :] = tail

    r, c = _iota2((CHUNK, CHUNK))
    tril = c <= r
    strict = c < r
    tril_b = tril.astype(F32).astype(BF16)
    triu_b = (r <= c).astype(F32).astype(BF16)
    dtb_row, alog_row = rowc_ref[0:1, :], rowc_ref[1:2, :]
    dtb_col, alog_col = colc_ref[0, :, 0:CHUNK], colc_ref[1, :, 0:CHUNK]

    items = [(ci, h) for ci in range(nc) for h in range(H_A)]
    rows = [slice(ci * CHUNK, (ci + 1) * CHUNK) for ci in range(nc)]
    sm = sm_ref[0]
    beta_all = _sigmoid(sm)
    g_all = -jnp.exp(alog_row) * _softplus(sm + dtb_row)
    gc_all = [_cum_left(tril_b, g_all[rows[ci]]) for ci in range(nc)]
    gct = [_cum_right(-jnp.exp(alog_col) * _softplus(smt_ref[0, ci] + dtb_col), triu_b) for ci in range(nc)]

    def l2n(x):
        return x * lax.rsqrt(jnp.sum(x * x, axis=-1, keepdims=True) + EPS)

    qs = [l2n(act[rows[ci], h * DK_A:(h + 1) * DK_A]) * DK_A ** -0.5 for ci, h in items]
    ks = [l2n(act[rows[ci], (H_A + h) * DK_A:(H_A + h + 1) * DK_A]) for ci, h in items]
    vs = [act[rows[ci], 2 * H_A * DK_A + h * DV_A:2 * H_A * DK_A + (h + 1) * DV_A] for ci, h in items]
    bcol = [beta_all[rows[ci], h:h + 1] for ci, h in items]
    gcol = [gc_all[ci][:, H_A + h:H_A + h + 1] for ci, h in items]
    dm = [jnp.exp(jnp.where(tril, gcol[n] - gct[ci][H_A + h:H_A + h + 1, :], NEG_BIG))
          for n, (ci, h) in enumerate(items)]
    kb = [k * b for k, b in zip(ks, bcol)]
    kbf = [k.astype(BF16) for k in ks]
    ms = [_mm_nt(a, b) * jnp.where(strict, d, 0.0) for a, b, d in zip(kb, kbf, dm)]
    tinv = _unit_lower_inverse(ms, r, c)
    egc = [jnp.exp(g) for g in gcol]
    uw = [_mm(t, jnp.concatenate([v * b, x * e], axis=1)) for t, v, b, x, e in zip(tinv, vs, bcol, kb, egc)]
    attn = [(_mm_nt(q, k) * d).astype(BF16) for q, k, d in zip(qs, kbf, dm)]
    gl = [g[CHUNK - 1:CHUNK, :] for g in gcol]
    wq = [jnp.concatenate([x[:, DV_A:], q * e], axis=0).astype(BF16) for x, q, e in zip(uw, qs, egc)]
    kgt = [jnp.transpose(k * jnp.exp(l - g)).astype(BF16) for k, l, g in zip(ks, gl, gcol)]
    egl = [jnp.exp(l) for l in gl]

    s = [s_scr[h] for h in range(H_A)]
    for ci in range(nc):
        idx = [ci * H_A + h for h in range(H_A)]
        ws = [_mm(wq[n], s[h]) for h, n in enumerate(idx)]
        v_new = [uw[n][:, :DV_A] - ws[h][:CHUNK] for h, n in enumerate(idx)]
        o = [ws[h][CHUNK:] + _mm(attn[n], v_new[h]) for h, n in enumerate(idx)]
        s = [s[h] * egl[n] + _mm(kgt[n], v_new[h]) for h, n in enumerate(idx)]
        for h in range(H_A):
            lanes = slice(h * DV_A, (h + 1) * DV_A)
            gated = _rms(o[h], gn_ref[...]) * _silu(za_ref[0, rows[ci], lanes])
            o_ref[0, rows[ci], lanes] = gated.astype(BF16)
    for h in range(H_A):
        s_scr[h] = s[h]
    sout_ref[0] = s_scr[...]


def _gdn(qkv, small, za, conv_w, conv_buf, s_delta, a_log, dt_bias, g_norm_a, nc):
    B, T, _ = qkv.shape
    tc = nc * CHUNK
    smt = small[..., :8].reshape(B, T // CHUNK, CHUNK, 8).transpose(0, 1, 3, 2)
    lane = jnp.zeros((LANES,), F32)
    rowc = jnp.stack([lane.at[H_A:2 * H_A].set(dt_bias), lane.at[H_A:2 * H_A].set(a_log)])
    col = jnp.zeros((8,), F32)
    colc = jnp.stack([col.at[H_A:2 * H_A].set(dt_bias), col.at[H_A:2 * H_A].set(a_log)])
    colc = jnp.broadcast_to(colc[:, :, None], (2, 8, LANES))
    cbuf = jnp.pad(conv_buf, ((0, 0), (8 - (CONV_W - 1), 0), (0, 0)))
    tok = lambda n: pl.BlockSpec((1, tc, n), lambda b, i: (b, i, 0))
    full = lambda a: pl.BlockSpec(a.shape, lambda b, i: (0,) * a.ndim)
    o, s_new, ctail = pl.pallas_call(
        functools.partial(_gdn_kernel, nc),
        grid=(B, T // tc),
        in_specs=[tok(QKV_A), tok(LANES), pl.BlockSpec((1, nc, 8, CHUNK), lambda b, i: (b, i, 0, 0)), tok(W_A),
                  full(conv_w), pl.BlockSpec((1, 8, QKV_A), lambda b, i: (b, 0, 0)),
                  pl.BlockSpec((1, H_A, DK_A, DV_A), lambda b, i: (b, 0, 0, 0)),
                  full(rowc), full(colc), full(g_norm_a)],
        out_specs=[tok(W_A), pl.BlockSpec((1, H_A, DK_A, DV_A), lambda b, i: (b, 0, 0, 0)),
                   pl.BlockSpec((1, 8, QKV_A), lambda b, i: (b, 0, 0))],
        out_shape=[jax.ShapeDtypeStruct((B, T, W_A), BF16), jax.ShapeDtypeStruct((B, H_A, DK_A, DV_A), F32),
                   jax.ShapeDtypeStruct((B, 8, QKV_A), F32)],
        scratch_shapes=[pltpu.VMEM((tc + 8, QKV_A), F32), pltpu.VMEM((tc, QKV_A), F32),
                        pltpu.VMEM((H_A, DK_A, DV_A), F32)],
        compiler_params=_cparams(2),
        name="gdn",
    )(qkv, small, smt, za, conv_w, cbuf, s_delta, rowc, colc, g_norm_a)
    return o, s_new, ctail[:, 8 - (CONV_W - 1):, :]


def _attn_kernel(tq, q_ref, k_ref, v_ref, zb_ref, o_ref, m_scr, acc_scr):
    i = pl.program_id(2)
    heads = (0, 1)
    hl = [slice(hh * LANES, (hh + 1) * LANES) for hh in heads]
    r, c = _iota2((tq, tq))
    mask = (c >> 6) <= (r >> 6)
    qs = [q_ref[0, :, hl[hh]] for hh in heads]
    m_scr[...] = jnp.full((2, tq, LANES), NEG_BIG, F32)
    acc_scr[...] = jnp.zeros((2, tq, LANES), F32)

    def tile(rows, tk, masked):
        s = [_mm_nt(qs[hh], k_ref[0, rows, hl[hh]]) for hh in heads]
        if masked:
            s = [jnp.where(mask, x, NEG_BIG) for x in s]
        m_prev = [m_scr[hh] for hh in heads]
        m_new = [jnp.maximum(m_prev[hh], jnp.max(s[hh], axis=-1, keepdims=True)) for hh in heads]
        alpha = [jnp.exp2(m_prev[hh] - m_new[hh]) for hh in heads]
        p = [jnp.exp2(s[hh] - jnp.concatenate([m_new[hh]] * (tk // LANES), axis=1)) for hh in heads]
        pv = [_mm(p[hh], v_ref[0, rows, hl[hh]]) for hh in heads]
        for hh in heads:
            acc_scr[hh] = alpha[hh] * acc_scr[hh] + pv[hh]
            m_scr[hh] = m_new[hh]

    def body(j, carry):
        tile(pl.ds(pl.multiple_of(j * 2 * tq, tq), 2 * tq), 2 * tq, False)
        return carry
    lax.fori_loop(0, i // 2, body, 0)

    @pl.when(i % 2 == 1)
    def _():
        tile(pl.ds(pl.multiple_of((i - 1) * tq, tq), tq), tq, False)

    tile(pl.ds(pl.multiple_of(i * tq, tq), tq), tq, True)
    _, lane = _iota2((tq, LANES))
    a0, a1 = acc_scr[0], acc_scr[1]
    o0 = a0 / pltpu.roll(a0, DV_B, 1)
    o1 = a1 / pltpu.roll(a1, DV_B, 1)
    o = jnp.where(lane < DV_B, o0, pltpu.roll(o1, DV_B, 1))
    o_ref[0] = (o * _silu(zb_ref[0])).astype(BF16)


def _attention(qp, kp, vp, zb, tq):
    B, T, _ = qp.shape
    pair = lambda rows, im: pl.BlockSpec((1, rows, 2 * LANES), im)
    return pl.pallas_call(
        functools.partial(_attn_kernel, tq),
        grid=(B, H_B // 2, T // tq),
        in_specs=[pair(tq, lambda b, h, i: (b, i, h)), pair(T, lambda b, h, i: (b, 0, h)),
                  pair(T, lambda b, h, i: (b, 0, h)), pl.BlockSpec((1, tq, LANES), lambda b, h, i: (b, i, h))],
        out_specs=pl.BlockSpec((1, tq, LANES), lambda b, h, i: (b, i, h)),
        out_shape=jax.ShapeDtypeStruct((B, T, W_B), BF16),
        scratch_shapes=[pltpu.VMEM((2, tq, LANES), F32)] * 2,
        compiler_params=_cparams(3),
        name="attn",
    )(qp, kp, vp, zb)


def _attn_decode_kernel(q_ref, cpast_ref, krpast_ref, cnew_ref, small_ref, wx_ref, wuv_ref, zb_ref, o_ref):
    t = q_ref.shape[1]
    q = q_ref[0]
    qx = jnp.concatenate([_mm(q[:, h * LANES:(h + 1) * LANES], wx_ref[h]) for h in range(H_B)], axis=0)
    ql, qr = qx[:, :KV_LORA].astype(BF16), qx[:, KV_LORA:].astype(BF16)
    segs = [(cpast_ref[0].astype(BF16), krpast_ref[0].astype(BF16)),
            (cnew_ref[0].astype(BF16), small_ref[0].astype(BF16))]
    s = [_mm_nt(ql, c) + _mm_nt(qr, kr) for c, kr in segs]
    m = jnp.maximum(jnp.max(s[0], axis=-1, keepdims=True), jnp.max(s[1], axis=-1, keepdims=True))
    p = [jnp.exp2(x - m) for x in s]
    l = jnp.sum(p[0], axis=-1, keepdims=True) + jnp.sum(p[1], axis=-1, keepdims=True)
    ol = (_mm(p[0], segs[0][0]) + _mm(p[1], segs[1][0])) / l
    o = _mm(ol[0:t], wuv_ref[0])
    for h in range(1, H_B):
        o = o + _mm(ol[h * t:(h + 1) * t], wuv_ref[h])
    o_ref[0] = (o * _silu(zb_ref[0])).astype(BF16)


def _attn_decode(qp, ckv_past, kr_past, ckv_new, small_new, wts, zb):
    B, T, _ = qp.shape
    P = ckv_past.shape[1]
    krp = jnp.pad(kr_past, ((0, 0), (0, 0), (KR_LANE, LANES - KR_LANE - D_ROPE)))
    row = lambda r, n: pl.BlockSpec((1, r, n), lambda b: (b, 0, 0))
    full = lambda a: pl.BlockSpec(a.shape, lambda b: (0,) * a.ndim)
    return pl.pallas_call(
        _attn_decode_kernel,
        grid=(B,),
        in_specs=[row(T, H_B * LANES), row(P, KV_LORA), row(P, LANES), row(T, KV_LORA), row(T, LANES),
                  full(wts["wx"]), full(wts["wuv"]), row(T, W_B)],
        out_specs=row(T, W_B),
        out_shape=jax.ShapeDtypeStruct((B, T, W_B), BF16),
        compiler_params=_cparams(1),
        name="attn_decode",
    )(qp, ckv_past, krp, ckv_new, small_new, wts["wx"], wts["wuv"], zb)


def _out_kernel(n_in, *refs):
    ins = refs[:n_in]
    ws = refs[n_in:2 * n_in]
    x_ref, mod_ref, gpost_ref, o_ref = refs[2 * n_in:]
    acc = _mm(ins[0][0], ws[0][...])
    for a, w in zip(ins[1:], ws[1:]):
        acc = acc + _mm(a[0], w[...])
    o_ref[0] = x_ref[0] + mod_ref[0, 2:3, :] * _rms(acc, gpost_ref[...])


def _out_proj(ins, ws, x, mod, g_post, tm):
    B, T, _ = x.shape
    tok = lambda n: pl.BlockSpec((1, tm, n), lambda b, i: (b, i, 0))
    full = lambda a: pl.BlockSpec(a.shape, lambda b, i: (0,) * a.ndim)
    return pl.pallas_call(
        functools.partial(_out_kernel, len(ins)),
        grid=(B, T // tm),
        in_specs=[tok(a.shape[-1]) for a in ins] + [full(w) for w in ws]
        + [tok(D), pl.BlockSpec((1, 3, D), lambda b, i: (b, 0, 0)), full(g_post)],
        out_specs=tok(D),
        out_shape=jax.ShapeDtypeStruct((B, T, D), F32),
        compiler_params=_cparams(2),
        name="out_proj",
    )(*ins, *ws, x, mod, g_post)


def _odd_layer_kernel(layer, nc, x_ref, mod_ref, gpre_ref, w_ref, lbl_ref, s0_ref, gn_ref, wout_ref, gpost_ref,
                      xo_ref, sout_ref, qs_ref, lf_ref, iv_ref, z_ref, o_ref, s_scr, *bufs):
    @pl.when(pl.program_id(1) == 0)
    def _():
        s_scr[...] = s0_ref[0]

    x = x_ref[0]
    hb = _prenorm(x, mod_ref, gpre_ref).astype(BF16)
    proj = jnp.dot(hb, w_ref[...], preferred_element_type=F32)
    n = H_C * DK_C
    lg = lbl_ref[...]
    e = jnp.exp(lg - jnp.max(lg, axis=0, keepdims=True))
    p = e / jnp.sum(e, axis=0, keepdims=True)
    lb = jnp.sum(p[1:layer + 1], axis=0, keepdims=True)
    fg = lb + (1.0 - lb) * _sigmoid(proj[:, n:2 * n])
    qs_ref[...] = _silu(proj[:, :n])
    lf_ref[...] = jnp.log(fg)
    iv_ref[...] = proj[:, 2 * n:2 * n + W_C]
    z_ref[...] = proj[:, 2 * n + W_C:]

    nsub = CHUNK // SUB
    r2, c2 = _iota2((CHUNK, CHUNK))
    cum_l = (c2 <= r2).astype(F32).astype(BF16)
    rm, cm = _iota2((CHUNK, KS_ROWS))
    seg = jnp.zeros_like(cm)
    off = jnp.zeros_like(cm)
    for si in range(1, nsub):
        start = SUB * si * (si + 1) // 2
        seg = seg + (cm >= start).astype(jnp.int32)
        off = off + jnp.where(cm >= start, SUB * si, 0)
    pmask = ((rm >> 4) == seg) & ((cm - off) <= rm) & (cm < KS_USED)

    heads = range(H_C)
    hl = [slice(h * DK_C, (h + 1) * DK_C) for h in heads]

    def prep(ci, buf):
        qt_s, qg_s, kt_s, ks_s, it_s, dec_s = buf
        rows = pl.ds(pl.multiple_of(ci * CHUNK, CHUNK), CHUNK)
        lf = lf_ref[rows, :]
        qs = qs_ref[rows, :]
        ivb = iv_ref[rows, :].astype(BF16)
        g = _cum_left(cum_l, lf)
        wloc = jnp.concatenate([g[:SUB]] + [g[j * SUB:(j + 1) * SUB] - g[j * SUB - 1:j * SUB, :]
                                            for j in range(1, CHUNK // SUB)], axis=0)
        k = 1.0 - jnp.exp(lf)
        sub = [slice(j * SUB, (j + 1) * SUB) for j in range(nsub)]
        bound = [g[(j + 1) * SUB - 1:(j + 1) * SUB, :] for j in range(nsub)]
        diag = (k * jnp.exp(jnp.minimum(-wloc, EXP_CLAMP))).astype(BF16)
        base = [k[sub[j]] * jnp.exp(wloc[(j + 1) * SUB - 1:(j + 1) * SUB, :] - wloc[sub[j]]) for j in range(nsub - 1)]
        pieces = []
        for si in range(nsub):
            for j in range(si):
                far = base[j] if j + 1 == si else base[j] * jnp.exp(bound[si - 1] - bound[j])
                pieces.append(far.astype(BF16))
            pieces.append(diag[sub[si]])
        pieces.append(jnp.zeros((KS_ROWS - KS_USED, W_C), BF16))
        ks_s[...] = jnp.concatenate(pieces, axis=0)
        qt_s[...] = (qs * jnp.exp(wloc)).astype(BF16)
        qg_s[...] = (qs * jnp.exp(g)).astype(BF16)
        gl = g[CHUNK - 1:CHUNK, :]
        kt_s[...] = (k * jnp.exp(gl - g)).astype(BF16)
        dec_s[...] = jnp.broadcast_to(jnp.exp(gl), dec_s.shape)
        it_s[...] = jnp.concatenate([ivb[:(si + 1) * SUB] for si in range(nsub)]
                                    + [jnp.zeros((KS_ROWS - KS_USED, W_C), BF16)], axis=0)

    def mat(ci, buf):
        qt_s, qg_s, kt_s, ks_s, it_s, dec_s = buf
        rows = pl.ds(pl.multiple_of(ci * CHUNK, CHUNK), CHUNK)
        last = slice(KS_USED - CHUNK, KS_USED)
        st = [s_scr[h] for h in heads]
        sc = [_mm_nt(qt_s[:, hl[h]], ks_s[:, hl[h]]) for h in heads]
        upd = [_mm_tn(it_s[last, hl[h]], kt_s[:, hl[h]]) for h in heads]
        inter = [_mm_nt(qg_s[:, hl[h]], st[h]) for h in heads]
        p = [jnp.where(pmask, x, 0.0).astype(BF16) for x in sc]
        o = [_mm(p[h], it_s[:, hl[h]]) + inter[h] for h in heads]
        for h in heads:
            s_scr[h] = st[h] * dec_s[0:1, hl[h]] + upd[h]
            o_ref[rows, hl[h]] = _rms(o[h], gn_ref[...]) * _silu(z_ref[rows, hl[h]])

    buf_a, buf_b = bufs[:6], bufs[6:]
    prep(0, buf_a)
    if nc == 1:
        mat(0, buf_a)
    else:
        def pair(i, carry):
            c0 = 2 * i
            prep(c0 + 1, buf_b)
            mat(c0, buf_a)
            prep(jnp.minimum(c0 + 2, nc - 1), buf_a)
            mat(c0 + 1, buf_b)
            return carry
        lax.fori_loop(0, nc // 2, pair, 0)
    sout_ref[0] = s_scr[...]
    acc = _mm(o_ref[...], wout_ref[...])
    xo_ref[0] = x + mod_ref[0, 2:3, :] * _rms(acc, gpost_ref[...])


def _odd_layer(x, mod, g_pre, g_post, w_in, w_out, lb_logits, s_hgrn, g_norm_c, layer, nc):
    B, T, _ = x.shape
    assert nc == 1 or nc % 2 == 0
    tc = nc * CHUNK
    operand_bufs = [pltpu.VMEM((CHUNK, W_C), BF16)] * 3 + [pltpu.VMEM((KS_ROWS, W_C), BF16)] * 2 + [
        pltpu.VMEM((8, W_C), F32)]
    tok = pl.BlockSpec((1, tc, D), lambda b, i: (b, i, 0))
    st = pl.BlockSpec((1, H_C, DV_C, DK_C), lambda b, i: (b, 0, 0, 0))
    full = lambda a: pl.BlockSpec(a.shape, lambda b, i: (0,) * a.ndim)
    x_new, s_new = pl.pallas_call(
        functools.partial(_odd_layer_kernel, layer, nc),
        grid=(B, T // tc),
        in_specs=[tok, pl.BlockSpec((1, 3, D), lambda b, i: (b, 0, 0)), full(g_pre), full(w_in), full(lb_logits),
                  st, full(g_norm_c), full(w_out), full(g_post)],
        out_specs=[tok, st],
        out_shape=[jax.ShapeDtypeStruct((B, T, D), F32), jax.ShapeDtypeStruct((B, H_C, DV_C, DK_C), F32)],
        scratch_shapes=[pltpu.VMEM((tc, W_C), F32)] * 5 + [pltpu.VMEM((H_C, DV_C, DK_C), F32)] + operand_bufs * 2,
        compiler_params=_cparams(2),
        name="odd_layer",
    )(x, mod, g_pre, w_in, lb_logits, jnp.swapaxes(s_hgrn, -1, -2), g_norm_c, w_out, g_post)
    return x_new, jnp.swapaxes(s_new, -1, -2)


def _even_weights(w_in, g_q, w_uq, g_kv, w_ukv):
    o_b, o_a, o_za = QKV_A, QKV_A + H_A, QKV_A + 2 * H_A
    o_cq = o_za + W_A
    o_ckv = o_cq + Q_LORA
    o_kr = o_ckv + KV_LORA
    o_zb = o_kr + D_ROPE
    half = D_ROPE // 2
    wbig = jnp.concatenate([w_in[:, :QKV_A], w_in[:, o_za:o_cq], w_in[:, o_cq:o_ckv], w_in[:, o_ckv:o_kr],
                            w_in[:, o_zb:]], axis=1)
    kr = w_in[:, o_kr:o_zb]
    z = lambda n: jnp.zeros((D, n), F32)
    wsm = jnp.concatenate([w_in[:, o_b:o_za], kr, z(LANES - KR_LANE - D_ROPE),
                           z(KR_LANE), -kr[:, half:], kr[:, :half], z(LANES - KR_LANE - D_ROPE)], axis=1)
    w3 = w_uq.reshape(Q_LORA, H_B, D_NOPE + D_ROPE)
    nope, r1, r2 = w3[..., :D_NOPE], w3[..., D_NOPE:D_NOPE + half], w3[..., D_NOPE + half:]
    zq = lambda n: jnp.zeros((Q_LORA, H_B, n), F32)
    pad = LANES - D_NOPE - D_ROPE
    wq = jnp.concatenate([jnp.concatenate([nope, r1, r2, zq(pad)], -1).reshape(Q_LORA, H_B * LANES),
                          jnp.concatenate([zq(D_NOPE), -r2, r1, zq(pad)], -1).reshape(Q_LORA, H_B * LANES)], axis=1)
    k3 = w_ukv.reshape(KV_LORA, H_B, D_NOPE + DV_B)
    wk = jnp.concatenate([k3[..., :D_NOPE], jnp.zeros((KV_LORA, H_B, LANES - D_NOPE), F32)], -1)
    wv = jnp.concatenate([k3[..., D_NOPE:], jnp.zeros((KV_LORA, H_B, LANES - DV_B), F32)], -1)
    wkv = jnp.concatenate([wk.reshape(KV_LORA, H_B * LANES), wv.reshape(KV_LORA, H_B * LANES)], axis=1)
    j = jnp.arange(D_ROPE)
    pm = jnp.zeros((LANES, H_B, LANES), F32).at[KR_LANE + j, :, D_NOPE + j].set(1.0).reshape(LANES, H_B * LANES)
    wx = jnp.zeros((H_B, LANES, KV_LORA + LANES), F32)
    wx = wx.at[:, :D_NOPE, :KV_LORA].set(jnp.transpose(k3[..., :D_NOPE], (1, 2, 0)))
    wx = wx.at[:, D_NOPE + j, KV_LORA + KR_LANE + j].set(1.0)
    hh = jnp.arange(H_B)
    wuv = jnp.zeros((H_B, KV_LORA, H_B, DV_B), F32).at[hh, :, hh, :].set(jnp.transpose(k3[..., D_NOPE:], (1, 0, 2)))
    wuv = wuv.reshape(H_B, KV_LORA, W_B)
    return dict(wbig=wbig.astype(BF16), wsm=wsm.astype(BF16), wq=wq.astype(BF16), wkv=wkv.astype(BF16),
                pm=pm.astype(BF16), gq=g_q.reshape(1, Q_LORA), gkv=g_kv.reshape(1, KV_LORA),
                wx=wx.astype(BF16), wuv=wuv.astype(BF16))


def _rope_tables(pos):
    half = D_ROPE // 2
    inv = ROPE_THETA ** (-jnp.arange(half, dtype=F32) / half)
    ang = pos.astype(F32)[:, None] * inv[None, :]
    cos, sin = jnp.cos(ang), jnp.sin(ang)
    t = pos.shape[0]
    one, zero = jnp.ones, jnp.zeros
    scale = (D_NOPE + D_ROPE) ** -0.5 * math.log2(math.e)
    pad_q = LANES - D_NOPE - D_ROPE
    pad_k = LANES - KR_LANE - D_ROPE
    cq = scale * jnp.concatenate([one((t, D_NOPE), F32), cos, cos, zero((t, pad_q), F32)], axis=1)
    sq = scale * jnp.concatenate([zero((t, D_NOPE), F32), sin, sin, zero((t, pad_q), F32)], axis=1)
    ck = jnp.concatenate([one((t, KR_LANE), F32), cos, cos, zero((t, pad_k), F32)], axis=1)
    sk = jnp.concatenate([zero((t, KR_LANE), F32), sin, sin, zero((t, pad_k), F32)], axis=1)
    return cq, sq, ck, sk


def _tiles(T):
    tm = min(256, T)
    nc = min(4, T // CHUNK)
    tq = min(512, T)
    return tm, nc, tq


def _trunk(x, mod, ckv_past, kr_past, s_delta, s_conv, s_hgrn, prm, tiles=None):
    B, T, _ = x.shape
    tm, nc, tq = tiles or _tiles(T)
    past = 0 if ckv_past is None else ckv_past.shape[2]
    tabs = _rope_tables(past + jnp.arange(T, dtype=jnp.int32))
    nd, ncv, nk, nr, nh = [], [], [], [], []
    for l in range(DEPTH):
        m = mod[l].reshape(B, 3, D)
        g_pre, g_post = prm["g_pre"][l].reshape(1, D), prm["g_post"][l].reshape(1, D)
        if l % 2 == 0:
            e = l // 2
            wts = prm["even"][e]
            qkv, za, zb, small, ckv, qp, kp, vp = _even_in(x, m, g_pre, wts, tabs, tm)
            o_a, s_new, conv_new = _gdn(qkv, small, za, prm["conv_w"][e], s_conv[e], s_delta[e], prm["a_log"][e],
                                        prm["dt_bias"][e], prm["g_norm_a"][e].reshape(1, DV_A), nc)
            if past:
                assert past % CHUNK == 0 and T <= CHUNK
                o_b = _attn_decode(qp, ckv_past[e], kr_past[e], ckv, small, wts, zb)
            else:
                o_b = _attention(qp, kp, vp, zb, tq)
            w_out = prm["w_out_e"][e]
            x = _out_proj([o_a, o_b], [w_out[:W_A], w_out[W_A:]], x, m, g_post, tm)
            nd.append(s_new)
            ncv.append(conv_new)
            nk.append(ckv)
            nr.append(small[..., KR_LANE:KR_LANE + D_ROPE])
        else:
            j = l // 2
            x, s_new = _odd_layer(x, m, g_pre, g_post, prm["w_in_o"][j], prm["w_out_o"][j], prm["lb_logits"],
                                  s_hgrn[j], prm["g_norm_c"][j].reshape(1, DV_C), l, min(8, T // CHUNK))
            nh.append(s_new)
    return (x, jnp.stack(nd), jnp.stack(ncv), jnp.stack(nk), jnp.stack(nr), jnp.stack(nh))


def _forward(x_prompt, x_sample, c_prompt, c_sample, cache_ckv, cache_kr, state_delta, state_conv, state_hgrn,
             w_ada, b_ada, g_pre, g_post, w_in_e, conv_w, a_log, dt_bias, g_norm_a, g_q, w_uq, g_kv, w_ukv,
             w_out_e, w_in_o, lb_logits, g_norm_c, w_out_o, tiles_prompt=None, tiles_sample=None):
    bp, bs = x_prompt.shape[0], x_sample.shape[0]
    n_even, n_odd = w_in_e.shape[0], w_in_o.shape[0]
    mod = _ada(jnp.concatenate([c_prompt, c_sample], axis=0), w_ada, b_ada)
    prm = dict(
        g_pre=g_pre, g_post=g_post, conv_w=conv_w, a_log=a_log, dt_bias=dt_bias, g_norm_a=g_norm_a,
        g_norm_c=g_norm_c, lb_logits=lb_logits,
        even=[_even_weights(w_in_e[e], g_q[e], w_uq[e], g_kv[e], w_ukv[e]) for e in range(n_even)],
        w_out_e=w_out_e.astype(BF16), w_in_o=w_in_o.astype(BF16), w_out_o=w_out_o.astype(BF16))
    zeros = lambda *s: jnp.zeros(s, F32)
    outs_p = _trunk(x_prompt, mod[:, :bp], None, None, zeros(n_even, bp, H_A, DK_A, DV_A),
                    zeros(n_even, bp, CONV_W - 1, QKV_A), zeros(n_odd, bp, H_C, DK_C, DV_C), prm, tiles_prompt)
    outs_s = _trunk(x_sample, mod[:, bp:], cache_ckv, cache_kr, state_delta, state_conv, state_hgrn, prm,
                    tiles_sample)
    return (outs_p[0], outs_s[0]) + outs_p[1:] + outs_s[1:]


def kernel(x_prompt, x_sample, c_prompt, c_sample, cache_ckv, cache_kr, state_delta, state_conv, state_hgrn,
           w_ada, b_ada, g_pre, g_post, w_in_e, conv_w, a_log, dt_bias, g_norm_a, g_q, w_uq, g_kv, w_ukv,
           w_out_e, w_in_o, lb_logits, g_norm_c, w_out_o):
    return _forward(x_prompt, x_sample, c_prompt, c_sample, cache_ckv, cache_kr, state_delta, state_conv,
                    state_hgrn, w_ada, b_ada, g_pre, g_post, w_in_e, conv_w, a_log, dt_bias, g_norm_a, g_q, w_uq,
                    g_kv, w_ukv, w_out_e, w_in_o, lb_logits, g_norm_c, w_out_o)
```

```python
import functools
import math

import jax
import jax.numpy as jnp
from jax import lax
from jax.experimental import pallas as pl
from jax.experimental.pallas import tpu as pltpu

F32 = jnp.float32
BF16 = jnp.bfloat16

D = 1024
DEPTH = 4
CHUNK = 64
EPS = 1e-6
H_A, DK_A, DV_A, CONV_W = 4, 128, 128, 4
H_B, Q_LORA, KV_LORA, D_NOPE, D_ROPE, DV_B = 8, 384, 256, 64, 32, 64
ROPE_THETA = 10000.0
H_C, DK_C, DV_C = 8, 128, 128
W_A, W_B, W_C = H_A * DV_A, H_B * DV_B, H_C * DV_C
QKV_A = H_A * (2 * DK_A + DV_A)
LANES = 128
SUB = 16
KS_USED = SUB * (CHUNK // SUB) * (CHUNK // SUB + 1) // 2
KS_ROWS = 2 * LANES
EXP_CLAMP = 60.0
NEG_BIG = -1e30
VMEM_LIMIT = 56 * 1024 * 1024


def _cparams(n_axes):
    return pltpu.CompilerParams(dimension_semantics=("arbitrary",) * n_axes, vmem_limit_bytes=VMEM_LIMIT)


def _mm(a, b):
    return jnp.dot(a.astype(BF16), b.astype(BF16), preferred_element_type=F32)


def _mm_nt(a, b):
    return lax.dot_general(a.astype(BF16), b.astype(BF16), (((1,), (1,)), ((), ())), preferred_element_type=F32)


def _mm_tn(a, b):
    return lax.dot_general(a.astype(BF16), b.astype(BF16), (((0,), (0,)), ((), ())), preferred_element_type=F32)


def _split3(x):
    hi = x.astype(BF16)
    r1 = x - hi.astype(F32)
    mid = r1.astype(BF16)
    lo = (r1 - mid.astype(F32)).astype(BF16)
    return hi, mid, lo


def _cum_left(l01, x):
    hi, mid, lo = _split3(x)
    d = functools.partial(jnp.dot, preferred_element_type=F32)
    return d(l01, hi) + d(l01, mid) + d(l01, lo)


def _cum_right(x, r01):
    hi, mid, lo = _split3(x)
    d = functools.partial(jnp.dot, preferred_element_type=F32)
    return d(hi, r01) + d(mid, r01) + d(lo, r01)


def _sigmoid(x):
    return 1.0 / (1.0 + jnp.exp(-x))


def _silu(x):
    return x * _sigmoid(x)


def _softplus(x):
    return jnp.maximum(x, 0.0) + jnp.log(1.0 + jnp.exp(-jnp.abs(x)))


def _rms(x, g):
    return x * lax.rsqrt(jnp.mean(x * x, axis=-1, keepdims=True) + EPS) * g


def _prenorm(x, mod_ref, g_ref):
    return _rms(x, g_ref[...]) * (1.0 + mod_ref[0, 1:2, :]) + mod_ref[0, 0:1, :]


def _iota2(shape):
    return lax.broadcasted_iota(jnp.int32, shape, 0), lax.broadcasted_iota(jnp.int32, shape, 1)


def _ada_kernel(c_ref, w_ref, b_ref, o_ref):
    o_ref[0] = _mm(_silu(c_ref[...]), w_ref[0]) + b_ref[0]


def _ada(c, w_ada, b_ada):
    nb = c.shape[0]
    tn = 1024
    return pl.pallas_call(
        _ada_kernel,
        grid=(DEPTH, 3 * D // tn),
        in_specs=[pl.BlockSpec((nb, D), lambda l, j: (0, 0)),
                  pl.BlockSpec((1, D, tn), lambda l, j: (l, 0, j)),
                  pl.BlockSpec((1, 1, tn), lambda l, j: (l, 0, j))],
        out_specs=pl.BlockSpec((1, nb, tn), lambda l, j: (l, 0, j)),
        out_shape=jax.ShapeDtypeStruct((DEPTH, nb, 3 * D), F32),
        compiler_params=_cparams(2),
        name="ada_mod",
    )(c, w_ada, b_ada.reshape(DEPTH, 1, 3 * D))


NB_E = QKV_A + W_A + Q_LORA + KV_LORA + W_B
KR_LANE = 0
GATE_LANE = KR_LANE + D_ROPE


def _value_ones(n):
    lane = lax.broadcasted_iota(jnp.int32, (1, n), 1)
    return ((lane & (LANES - 1)) >= DV_B).astype(F32)


def _even_in_kernel(x_ref, mod_ref, gpre_ref, wbig_ref, wsm_ref, gq_ref, wq_ref, gkv_ref, wkv_ref,
                    cq_ref, sq_ref, ck_ref, sk_ref,
                    qkv_ref, za_ref, zb_ref, small_ref, ckv_ref, qp_ref, kp_ref, vp_ref, kr_ref):
    hb = _prenorm(x_ref[0], mod_ref, gpre_ref).astype(BF16)
    big = jnp.dot(hb, wbig_ref[...], preferred_element_type=F32)
    o = 0
    qkv_ref[0] = big[:, o:o + QKV_A]
    o += QKV_A
    za_ref[0] = big[:, o:o + W_A]
    o += W_A
    cq = big[:, o:o + Q_LORA]
    o += Q_LORA
    ckv = big[:, o:o + KV_LORA]
    o += KV_LORA
    zb_ref[0] = big[:, o:o + W_B]
    half = D_ROPE // 2

    def rotate_pairs(v, first):
        n = v.shape[1]
        lane = lax.broadcasted_iota(jnp.int32, v.shape, 1) & (LANES - 1)
        return jnp.where(lane < first + half, pltpu.roll(v, n - half, 1), pltpu.roll(v, half, 1))

    sm = jnp.dot(hb, wsm_ref[...], preferred_element_type=F32)
    small = sm * ck_ref[...] + rotate_pairs(sm, KR_LANE) * sk_ref[...]
    small_ref[0] = small
    kr_ref[0] = small[:, KR_LANE:KR_LANE + D_ROPE]
    q1 = _mm(_rms(cq, gq_ref[...]), wq_ref[...])
    nq = H_B * LANES
    cq_t = jnp.concatenate([cq_ref[...]] * H_B, axis=1)
    sq_t = jnp.concatenate([sq_ref[...]] * H_B, axis=1)
    qp_ref[0] = (q1 * cq_t + rotate_pairs(q1, D_NOPE) * sq_t).astype(BF16)
    cn = _rms(ckv, gkv_ref[...])
    ckv_ref[0] = cn
    kv2 = _mm(cn, wkv_ref[...])
    lane = lax.broadcasted_iota(jnp.int32, small.shape, 1)
    kpe = jnp.where((lane >= D_NOPE) & (lane < D_NOPE + D_ROPE), pltpu.roll(small, D_NOPE - KR_LANE, 1), 0.0)
    kp_ref[0] = (kv2[:, :nq] + jnp.concatenate([kpe] * H_B, axis=1)).astype(BF16)
    vp_ref[0] = (kv2[:, nq:] + _value_ones(nq)).astype(BF16)


def _even_in(x, mod, g_pre, wts, tabs, tm):
    B, T, _ = x.shape
    grid = (B, T // tm)
    full = lambda a: pl.BlockSpec(a.shape, lambda b, i: (0,) * a.ndim)
    tok = lambda n: pl.BlockSpec((1, tm, n), lambda b, i: (b, i, 0))
    tab = pl.BlockSpec((tm, LANES), lambda b, i: (i, 0))
    outs = [(QKV_A, F32), (W_A, F32), (W_B, F32), (LANES, F32), (KV_LORA, F32),
            (H_B * LANES, BF16), (H_B * LANES, BF16), (H_B * LANES, BF16), (D_ROPE, F32)]
    return pl.pallas_call(
        _even_in_kernel,
        grid=grid,
        in_specs=[tok(D), pl.BlockSpec((1, 3, D), lambda b, i: (b, 0, 0)), full(g_pre),
                  full(wts["wbig"]), full(wts["wsm"]), full(wts["gq"]), full(wts["wq"]), full(wts["gkv"]),
                  full(wts["wkv"]), tab, tab, tab, tab],
        out_specs=[tok(n) for n, _ in outs],
        out_shape=[jax.ShapeDtypeStruct((B, T, n), dt) for n, dt in outs],
        compiler_params=_cparams(2),
        name="even_in",
    )(x, mod, g_pre, wts["wbig"], wts["wsm"], wts["gq"], wts["wq"], wts["gkv"], wts["wkv"], *tabs)


def _unit_lower_inverse(ms, r, c):
    eye = (r == c).astype(F32)
    same16 = (r >> 4) == (c >> 4)
    same32 = (r >> 5) == (c >> 5)
    off16 = same32 & jnp.logical_not(same16)
    md = [jnp.where(same16, m, 0.0) for m in ms]
    m1 = [jnp.where(off16, m, 0.0).astype(BF16) for m in ms]
    m2 = [jnp.where(same32, 0.0, m).astype(BF16) for m in ms]
    p = [eye - x for x in md]
    q = [_mm(x, x) for x in md]
    for step in range(3):
        p = [a + _mm(a, b) for a, b in zip(p, q)]
        if step < 2:
            q = [_mm(b, b) for b in q]
    for mk in (m1, m2):
        t = [_mm(a, b) for a, b in zip(p, mk)]
        p = [a - _mm(b, a) for a, b in zip(p, t)]
    return p


def _gdn_kernel(nc, qkv_ref, sm_ref, smt_ref, za_ref, convw_ref, cbuf_ref, s0_ref, rowc_ref, colc_ref, gn_ref,
                o_ref, sout_ref, cout_ref, xpad, act, s_scr):
    tc = nc * CHUNK

    @pl.when(pl.program_id(1) == 0)
    def _():
        xpad[0:8, :] = cbuf_ref[0]
        s_scr[...] = s0_ref[0]

    xpad[8:8 + tc, :] = qkv_ref[0]
    w = convw_ref[...]
    y = (w[0:1] * xpad[5:5 + tc, :] + w[1:2] * xpad[6:6 + tc, :] + w[2:3] * xpad[7:7 + tc, :]
         + w[3:4] * xpad[8:8 + tc, :])
    act[...] = _silu(y)
    tail = xpad[tc:tc + 8, :]
    cout_ref[0] = tail
    xpad[0:8, :] = tail

    r, c = _iota2((CHUNK, CHUNK))
    tril = c <= r
    strict = c < r
    tril_b = tril.astype(F32).astype(BF16)
    triu_b = (r <= c).astype(F32).astype(BF16)
    dtb_row, alog_row = rowc_ref[0:1, :], rowc_ref[1:2, :]
    dtb_col, alog_col = colc_ref[0, :, 0:CHUNK], colc_ref[1, :, 0:CHUNK]

    items = [(ci, h) for ci in range(nc) for h in range(H_A)]
    rows = [slice(ci * CHUNK, (ci + 1) * CHUNK) for ci in range(nc)]
    sm = sm_ref[0]
    beta_all = _sigmoid(sm)
    g_all = -jnp.exp(alog_row) * _softplus(sm + dtb_row)
    gc_all = [_cum_left(tril_b, g_all[rows[ci]]) for ci in range(nc)]
    gct = [_cum_right(-jnp.exp(alog_col) * _softplus(smt_ref[0, ci] + dtb_col), triu_b) for ci in range(nc)]

    def l2n(x):
        return x * lax.rsqrt(jnp.sum(x * x, axis=-1, keepdims=True) + EPS)

    qs = [l2n(act[rows[ci], h * DK_A:(h + 1) * DK_A]) * DK_A ** -0.5 for ci, h in items]
    ks = [l2n(act[rows[ci], (H_A + h) * DK_A:(H_A + h + 1) * DK_A]) for ci, h in items]
    vs = [act[rows[ci], 2 * H_A * DK_A + h * DV_A:2 * H_A * DK_A + (h + 1) * DV_A] for ci, h in items]
    bcol = [beta_all[rows[ci], GATE_LANE + h:GATE_LANE + h + 1] for ci, h in items]
    gcol = [gc_all[ci][:, GATE_LANE + H_A + h:GATE_LANE + H_A + h + 1] for ci, h in items]
    dm = [jnp.exp(jnp.where(tril, gcol[n] - gct[ci][H_A + h:H_A + h + 1, :], NEG_BIG))
          for n, (ci, h) in enumerate(items)]
    kb = [k * b for k, b in zip(ks, bcol)]
    kbf = [k.astype(BF16) for k in ks]
    ms = [_mm_nt(a, b) * jnp.where(strict, d, 0.0) for a, b, d in zip(kb, kbf, dm)]
    tinv = _unit_lower_inverse(ms, r, c)
    egc = [jnp.exp(g) for g in gcol]
    uw = [_mm(t, jnp.concatenate([v * b, x * e], axis=1)) for t, v, b, x, e in zip(tinv, vs, bcol, kb, egc)]
    attn = [(_mm_nt(q, k) * d).astype(BF16) for q, k, d in zip(qs, kbf, dm)]
    gl = [g[CHUNK - 1:CHUNK, :] for g in gcol]
    wq = [jnp.concatenate([x[:, DV_A:], q * e], axis=0).astype(BF16) for x, q, e in zip(uw, qs, egc)]
    kgt = [jnp.transpose(k * jnp.exp(l - g)).astype(BF16) for k, l, g in zip(ks, gl, gcol)]
    egl = [jnp.exp(l) for l in gl]

    s = [s_scr[h] for h in range(H_A)]
    for ci in range(nc):
        idx = [ci * H_A + h for h in range(H_A)]
        ws = [_mm(wq[n], s[h]) for h, n in enumerate(idx)]
        v_new = [uw[n][:, :DV_A] - ws[h][:CHUNK] for h, n in enumerate(idx)]
        o = [ws[h][CHUNK:] + _mm(attn[n], v_new[h]) for h, n in enumerate(idx)]
        s = [s[h] * egl[n] + _mm(kgt[n], v_new[h]) for h, n in enumerate(idx)]
        for h in range(H_A):
            lanes = slice(h * DV_A, (h + 1) * DV_A)
            gated = _rms(o[h], gn_ref[...]) * _silu(za_ref[0, rows[ci], lanes])
            o_ref[0, rows[ci], lanes] = gated.astype(BF16)
    for h in range(H_A):
        s_scr[h] = s[h]
    sout_ref[0] = s_scr[...]


def _gdn(qkv, small, za, conv_w, conv_buf, s_delta, a_log, dt_bias, g_norm_a, nc):
    B, T, _ = qkv.shape
    tc = nc * CHUNK
    gates = small[..., GATE_LANE:GATE_LANE + 2 * H_A]
    smt = gates.reshape(B, T // CHUNK, CHUNK, 2 * H_A).transpose(0, 1, 3, 2)
    lane = jnp.zeros((LANES,), F32)
    a_lanes = slice(GATE_LANE + H_A, GATE_LANE + 2 * H_A)
    rowc = jnp.stack([lane.at[a_lanes].set(dt_bias), lane.at[a_lanes].set(a_log)])
    col = jnp.zeros((8,), F32)
    colc = jnp.stack([col.at[H_A:2 * H_A].set(dt_bias), col.at[H_A:2 * H_A].set(a_log)])
    colc = jnp.broadcast_to(colc[:, :, None], (2, 8, LANES))
    cbuf = jnp.pad(conv_buf, ((0, 0), (8 - (CONV_W - 1), 0), (0, 0)))
    tok = lambda n: pl.BlockSpec((1, tc, n), lambda b, i: (b, i, 0))
    full = lambda a: pl.BlockSpec(a.shape, lambda b, i: (0,) * a.ndim)
    o, s_new, ctail = pl.pallas_call(
        functools.partial(_gdn_kernel, nc),
        grid=(B, T // tc),
        in_specs=[tok(QKV_A), tok(LANES), pl.BlockSpec((1, nc, 8, CHUNK), lambda b, i: (b, i, 0, 0)), tok(W_A),
                  full(conv_w), pl.BlockSpec((1, 8, QKV_A), lambda b, i: (b, 0, 0)),
                  pl.BlockSpec((1, H_A, DK_A, DV_A), lambda b, i: (b, 0, 0, 0)),
                  full(rowc), full(colc), full(g_norm_a)],
        out_specs=[tok(W_A), pl.BlockSpec((1, H_A, DK_A, DV_A), lambda b, i: (b, 0, 0, 0)),
                   pl.BlockSpec((1, 8, QKV_A), lambda b, i: (b, 0, 0))],
        out_shape=[jax.ShapeDtypeStruct((B, T, W_A), BF16), jax.ShapeDtypeStruct((B, H_A, DK_A, DV_A), F32),
                   jax.ShapeDtypeStruct((B, 8, QKV_A), F32)],
        scratch_shapes=[pltpu.VMEM((tc + 8, QKV_A), F32), pltpu.VMEM((tc, QKV_A), F32),
                        pltpu.VMEM((H_A, DK_A, DV_A), F32)],
        compiler_params=_cparams(2),
        name="gdn",
    )(qkv, small, smt, za, conv_w, cbuf, s_delta, rowc, colc, g_norm_a)
    return o, s_new, ctail[:, 8 - (CONV_W - 1):, :]


def _attn_kernel(tq, q_ref, k_ref, v_ref, zb_ref, o_ref, m_scr, acc_scr):
    i = pl.program_id(2)
    heads = (0, 1)
    hl = [slice(hh * LANES, (hh + 1) * LANES) for hh in heads]
    r, c = _iota2((tq, tq))
    mask = (c >> 6) <= (r >> 6)
    qs = [q_ref[0, :, hl[hh]] for hh in heads]
    m_scr[...] = jnp.full((2, tq, LANES), NEG_BIG, F32)
    acc_scr[...] = jnp.zeros((2, tq, LANES), F32)

    def tile(rows, tk, masked):
        s = [_mm_nt(qs[hh], k_ref[0, rows, hl[hh]]) for hh in heads]
        if masked:
            s = [jnp.where(mask, x, NEG_BIG) for x in s]
        m_prev = [m_scr[hh] for hh in heads]
        m_new = [jnp.maximum(m_prev[hh], jnp.max(s[hh], axis=-1, keepdims=True)) for hh in heads]
        alpha = [jnp.exp2(m_prev[hh] - m_new[hh]) for hh in heads]
        p = [jnp.exp2(s[hh] - jnp.concatenate([m_new[hh]] * (tk // LANES), axis=1)) for hh in heads]
        pv = [_mm(p[hh], v_ref[0, rows, hl[hh]]) for hh in heads]
        for hh in heads:
            acc_scr[hh] = alpha[hh] * acc_scr[hh] + pv[hh]
            m_scr[hh] = m_new[hh]

    def body(j, carry):
        tile(pl.ds(pl.multiple_of(j * 2 * tq, tq), 2 * tq), 2 * tq, False)
        return carry
    lax.fori_loop(0, i // 2, body, 0)

    @pl.when(i % 2 == 1)
    def _():
        tile(pl.ds(pl.multiple_of((i - 1) * tq, tq), tq), tq, False)

    tile(pl.ds(pl.multiple_of(i * tq, tq), tq), tq, True)
    _, lane = _iota2((tq, LANES))
    a0, a1 = acc_scr[0], acc_scr[1]
    o0 = a0 / pltpu.roll(a0, DV_B, 1)
    o1 = a1 / pltpu.roll(a1, DV_B, 1)
    o = jnp.where(lane < DV_B, o0, pltpu.roll(o1, DV_B, 1))
    o_ref[0] = (o * _silu(zb_ref[0])).astype(BF16)


def _attention(qp, kp, vp, zb, tq):
    B, T, _ = qp.shape
    pair = lambda rows, im: pl.BlockSpec((1, rows, 2 * LANES), im)
    return pl.pallas_call(
        functools.partial(_attn_kernel, tq),
        grid=(B, H_B // 2, T // tq),
        in_specs=[pair(tq, lambda b, h, i: (b, i, h)), pair(T, lambda b, h, i: (b, 0, h)),
                  pair(T, lambda b, h, i: (b, 0, h)), pl.BlockSpec((1, tq, LANES), lambda b, h, i: (b, i, h))],
        out_specs=pl.BlockSpec((1, tq, LANES), lambda b, h, i: (b, i, h)),
        out_shape=jax.ShapeDtypeStruct((B, T, W_B), BF16),
        scratch_shapes=[pltpu.VMEM((2, tq, LANES), F32)] * 2,
        compiler_params=_cparams(3),
        name="attn",
    )(qp, kp, vp, zb)


def _attn_decode_kernel(q_ref, cpast_ref, krpast_ref, cnew_ref, small_ref, wx_ref, wuv_ref, zb_ref, o_ref):
    t = q_ref.shape[1]
    q = q_ref[0]
    qx = jnp.concatenate([_mm(q[:, h * LANES:(h + 1) * LANES], wx_ref[h]) for h in range(H_B)], axis=0)
    ql, qr = qx[:, :KV_LORA].astype(BF16), qx[:, KV_LORA:].astype(BF16)
    segs = [(cpast_ref[0].astype(BF16), krpast_ref[0].astype(BF16)),
            (cnew_ref[0].astype(BF16), small_ref[0].astype(BF16))]
    s = [_mm_nt(ql, c) + _mm_nt(qr, kr) for c, kr in segs]
    m = jnp.maximum(jnp.max(s[0], axis=-1, keepdims=True), jnp.max(s[1], axis=-1, keepdims=True))
    p = [jnp.exp2(x - m) for x in s]
    l = jnp.sum(p[0], axis=-1, keepdims=True) + jnp.sum(p[1], axis=-1, keepdims=True)
    ol = (_mm(p[0], segs[0][0]) + _mm(p[1], segs[1][0])) / l
    o = _mm(ol[0:t], wuv_ref[0])
    for h in range(1, H_B):
        o = o + _mm(ol[h * t:(h + 1) * t], wuv_ref[h])
    o_ref[0] = (o * _silu(zb_ref[0])).astype(BF16)


def _attn_decode(qp, ckv_past, kr_past, ckv_new, small_new, wts, zb):
    B, T, _ = qp.shape
    P = ckv_past.shape[1]
    krp = jnp.pad(kr_past, ((0, 0), (0, 0), (KR_LANE, LANES - KR_LANE - D_ROPE)))
    row = lambda r, n: pl.BlockSpec((1, r, n), lambda b: (b, 0, 0))
    full = lambda a: pl.BlockSpec(a.shape, lambda b: (0,) * a.ndim)
    return pl.pallas_call(
        _attn_decode_kernel,
        grid=(B,),
        in_specs=[row(T, H_B * LANES), row(P, KV_LORA), row(P, LANES), row(T, KV_LORA), row(T, LANES),
                  full(wts["wx"]), full(wts["wuv"]), row(T, W_B)],
        out_specs=row(T, W_B),
        out_shape=jax.ShapeDtypeStruct((B, T, W_B), BF16),
        compiler_params=_cparams(1),
        name="attn_decode",
    )(qp, ckv_past, krp, ckv_new, small_new, wts["wx"], wts["wuv"], zb)


def _odd_layer_kernel(layer, nc, x_ref, oa_ref, ob_ref, wa_ref, wb_ref, modp_ref, gpostp_ref,
                      mod_ref, gpre_ref, w_ref, lbl_ref, s0_ref, gn_ref, wout_ref, gpost_ref,
                      xo_ref, sout_ref, qs_ref, lf_ref, iv_ref, z_ref, o_ref, s_scr, *bufs):
    @pl.when(pl.program_id(1) == 0)
    def _():
        s_scr[...] = s0_ref[0]

    mix = _mm(oa_ref[0], wa_ref[...]) + _mm(ob_ref[0], wb_ref[...])
    x = x_ref[0] + modp_ref[0, 2:3, :] * _rms(mix, gpostp_ref[...])
    hb = _prenorm(x, mod_ref, gpre_ref).astype(BF16)
    proj = jnp.dot(hb, w_ref[...], preferred_element_type=F32)
    n = H_C * DK_C
    lg = lbl_ref[...]
    e = jnp.exp(lg - jnp.max(lg, axis=0, keepdims=True))
    p = e / jnp.sum(e, axis=0, keepdims=True)
    lb = jnp.sum(p[1:layer + 1], axis=0, keepdims=True)
    fg = lb + (1.0 - lb) * _sigmoid(proj[:, n:2 * n])
    qs_ref[...] = _silu(proj[:, :n])
    lf_ref[...] = jnp.log(fg)
    iv_ref[...] = proj[:, 2 * n:2 * n + W_C]
    z_ref[...] = proj[:, 2 * n + W_C:]

    nsub = CHUNK // SUB
    r2, c2 = _iota2((CHUNK, CHUNK))
    cum_l = (c2 <= r2).astype(F32).astype(BF16)
    rm, cm = _iota2((CHUNK, KS_ROWS))
    seg = jnp.zeros_like(cm)
    off = jnp.zeros_like(cm)
    for si in range(1, nsub):
        start = SUB * si * (si + 1) // 2
        seg = seg + (cm >= start).astype(jnp.int32)
        off = off + jnp.where(cm >= start, SUB * si, 0)
    pmask = ((rm >> 4) == seg) & ((cm - off) <= rm) & (cm < KS_USED)

    heads = range(H_C)
    hl = [slice(h * DK_C, (h + 1) * DK_C) for h in heads]

    def prep(ci, buf):
        qt_s, qg_s, kt_s, ks_s, it_s, dec_s = buf
        rows = pl.ds(pl.multiple_of(ci * CHUNK, CHUNK), CHUNK)
        lf = lf_ref[rows, :]
        qs = qs_ref[rows, :]
        ivb = iv_ref[rows, :].astype(BF16)
        g = _cum_left(cum_l, lf)
        wloc = jnp.concatenate([g[:SUB]] + [g[j * SUB:(j + 1) * SUB] - g[j * SUB - 1:j * SUB, :]
                                            for j in range(1, CHUNK // SUB)], axis=0)
        k = 1.0 - jnp.exp(lf)
        sub = [slice(j * SUB, (j + 1) * SUB) for j in range(nsub)]
        bound = [g[(j + 1) * SUB - 1:(j + 1) * SUB, :] for j in range(nsub)]
        diag = (k * jnp.exp(jnp.minimum(-wloc, EXP_CLAMP))).astype(BF16)
        base = [k[sub[j]] * jnp.exp(wloc[(j + 1) * SUB - 1:(j + 1) * SUB, :] - wloc[sub[j]]) for j in range(nsub - 1)]
        pieces = []
        for si in range(nsub):
            for j in range(si):
                far = base[j] if j + 1 == si else base[j] * jnp.exp(bound[si - 1] - bound[j])
                pieces.append(far.astype(BF16))
            pieces.append(diag[sub[si]])
        pieces.append(jnp.zeros((KS_ROWS - KS_USED, W_C), BF16))
        ks_s[...] = jnp.concatenate(pieces, axis=0)
        qt_s[...] = (qs * jnp.exp(wloc)).astype(BF16)
        qg_s[...] = (qs * jnp.exp(g)).astype(BF16)
        gl = g[CHUNK - 1:CHUNK, :]
        kt_s[...] = (k * jnp.exp(gl - g)).astype(BF16)
        dec_s[...] = jnp.broadcast_to(jnp.exp(gl), dec_s.shape)
        it_s[...] = jnp.concatenate([ivb[:(si + 1) * SUB] for si in range(nsub)]
                                    + [jnp.zeros((KS_ROWS - KS_USED, W_C), BF16)], axis=0)

    def mat(ci, buf):
        qt_s, qg_s, kt_s, ks_s, it_s, dec_s = buf
        rows = pl.ds(pl.multiple_of(ci * CHUNK, CHUNK), CHUNK)
        last = slice(KS_USED - CHUNK, KS_USED)
        st = [s_scr[h] for h in heads]
        sc = [_mm_nt(qt_s[:, hl[h]], ks_s[:, hl[h]]) for h in heads]
        upd = [_mm_tn(it_s[last, hl[h]], kt_s[:, hl[h]]) for h in heads]
        inter = [_mm_nt(qg_s[:, hl[h]], st[h]) for h in heads]
        p = [jnp.where(pmask, x, 0.0).astype(BF16) for x in sc]
        o = [_mm(p[h], it_s[:, hl[h]]) + inter[h] for h in heads]
        for h in heads:
            s_scr[h] = st[h] * dec_s[0:1, hl[h]] + upd[h]
            o_ref[rows, hl[h]] = _rms(o[h], gn_ref[...]) * _silu(z_ref[rows, hl[h]])

    buf_a, buf_b = bufs[:6], bufs[6:]
    prep(0, buf_a)
    if nc == 1:
        mat(0, buf_a)
    else:
        def pair(i, carry):
            c0 = 2 * i
            prep(c0 + 1, buf_b)
            mat(c0, buf_a)
            prep(jnp.minimum(c0 + 2, nc - 1), buf_a)
            mat(c0 + 1, buf_b)
            return carry
        lax.fori_loop(0, nc // 2, pair, 0)
    sout_ref[0] = s_scr[...]
    acc = _mm(o_ref[...], wout_ref[...])
    xo_ref[0] = x + mod_ref[0, 2:3, :] * _rms(acc, gpost_ref[...])


def _odd_layer(x, prev, mod, g_pre, g_post, w_in, w_out, lb_logits, s_hgrn, g_norm_c, layer, nc):
    oa, ob, wa, wb, modp, gpostp = prev
    B, T, _ = x.shape
    assert nc == 1 or nc % 2 == 0
    tc = nc * CHUNK
    operand_bufs = [pltpu.VMEM((CHUNK, W_C), BF16)] * 3 + [pltpu.VMEM((KS_ROWS, W_C), BF16)] * 2 + [
        pltpu.VMEM((8, W_C), F32)]
    tok = pl.BlockSpec((1, tc, D), lambda b, i: (b, i, 0))
    half = pl.BlockSpec((1, tc, W_A), lambda b, i: (b, i, 0))
    modspec = pl.BlockSpec((1, 3, D), lambda b, i: (b, 0, 0))
    st = pl.BlockSpec((1, H_C, DV_C, DK_C), lambda b, i: (b, 0, 0, 0))
    full = lambda a: pl.BlockSpec(a.shape, lambda b, i: (0,) * a.ndim)
    x_new, s_new = pl.pallas_call(
        functools.partial(_odd_layer_kernel, layer, nc),
        grid=(B, T // tc),
        in_specs=[tok, half, half, full(wa), full(wb), modspec, full(gpostp),
                  modspec, full(g_pre), full(w_in), full(lb_logits), st, full(g_norm_c), full(w_out), full(g_post)],
        out_specs=[tok, st],
        out_shape=[jax.ShapeDtypeStruct((B, T, D), F32), jax.ShapeDtypeStruct((B, H_C, DV_C, DK_C), F32)],
        scratch_shapes=[pltpu.VMEM((tc, W_C), F32)] * 5 + [pltpu.VMEM((H_C, DV_C, DK_C), F32)] + operand_bufs * 2,
        compiler_params=_cparams(2),
        name="odd_layer",
    )(x, oa, ob, wa, wb, modp, gpostp, mod, g_pre, w_in, lb_logits, jnp.swapaxes(s_hgrn, -1, -2), g_norm_c, w_out,
      g_post)
    return x_new, jnp.swapaxes(s_new, -1, -2)


def _even_weights(w_in, g_q, w_uq, g_kv, w_ukv):
    o_b, o_a, o_za = QKV_A, QKV_A + H_A, QKV_A + 2 * H_A
    o_cq = o_za + W_A
    o_ckv = o_cq + Q_LORA
    o_kr = o_ckv + KV_LORA
    o_zb = o_kr + D_ROPE
    half = D_ROPE // 2
    wbig = jnp.concatenate([w_in[:, :QKV_A], w_in[:, o_za:o_cq], w_in[:, o_cq:o_ckv], w_in[:, o_ckv:o_kr],
                            w_in[:, o_zb:]], axis=1)
    kr = w_in[:, o_kr:o_zb]
    z = lambda n: jnp.zeros((D, n), F32)
    wsm = jnp.concatenate([z(KR_LANE), kr, w_in[:, o_b:o_za], z(LANES - GATE_LANE - 2 * H_A)], axis=1)
    w3 = w_uq.reshape(Q_LORA, H_B, D_NOPE + D_ROPE)
    wq = jnp.concatenate([w3, jnp.zeros((Q_LORA, H_B, LANES - D_NOPE - D_ROPE), F32)], -1)
    wq = wq.reshape(Q_LORA, H_B * LANES)
    k3 = w_ukv.reshape(KV_LORA, H_B, D_NOPE + DV_B)
    wk = jnp.concatenate([k3[..., :D_NOPE], jnp.zeros((KV_LORA, H_B, LANES - D_NOPE), F32)], -1)
    wv = jnp.concatenate([k3[..., D_NOPE:], jnp.zeros((KV_LORA, H_B, LANES - DV_B), F32)], -1)
    wkv = jnp.concatenate([wk.reshape(KV_LORA, H_B * LANES), wv.reshape(KV_LORA, H_B * LANES)], axis=1)
    j = jnp.arange(D_ROPE)
    wx = jnp.zeros((H_B, LANES, KV_LORA + LANES), F32)
    wx = wx.at[:, :D_NOPE, :KV_LORA].set(jnp.transpose(k3[..., :D_NOPE], (1, 2, 0)))
    wx = wx.at[:, D_NOPE + j, KV_LORA + KR_LANE + j].set(1.0)
    hh = jnp.arange(H_B)
    wuv = jnp.zeros((H_B, KV_LORA, H_B, DV_B), F32).at[hh, :, hh, :].set(jnp.transpose(k3[..., D_NOPE:], (1, 0, 2)))
    wuv = wuv.reshape(H_B, KV_LORA, W_B)
    return dict(wbig=wbig.astype(BF16), wsm=wsm.astype(BF16), wq=wq.astype(BF16), wkv=wkv.astype(BF16),
                gq=g_q.reshape(1, Q_LORA), gkv=g_kv.reshape(1, KV_LORA),
                wx=wx.astype(BF16), wuv=wuv.astype(BF16))


def _rope_tables(pos):
    half = D_ROPE // 2
    inv = ROPE_THETA ** (-jnp.arange(half, dtype=F32) / half)
    ang = pos.astype(F32)[:, None] * inv[None, :]
    cos, sin = jnp.cos(ang), jnp.sin(ang)
    t = pos.shape[0]
    one, zero = jnp.ones, jnp.zeros
    scale = (D_NOPE + D_ROPE) ** -0.5 * math.log2(math.e)
    pad_q = LANES - D_NOPE - D_ROPE
    pad_k = LANES - KR_LANE - D_ROPE
    cq = scale * jnp.concatenate([one((t, D_NOPE), F32), cos, cos, zero((t, pad_q), F32)], axis=1)
    sq = scale * jnp.concatenate([zero((t, D_NOPE), F32), -sin, sin, zero((t, pad_q), F32)], axis=1)
    ck = jnp.concatenate([one((t, KR_LANE), F32), cos, cos, one((t, pad_k), F32)], axis=1)
    sk = jnp.concatenate([zero((t, KR_LANE), F32), -sin, sin, zero((t, pad_k), F32)], axis=1)
    return cq, sq, ck, sk


def _tiles(T):
    tm = min(256, T)
    nc = min(4, T // CHUNK)
    tq = min(512, T)
    return tm, nc, tq


def _trunk(x, mod, ckv_past, kr_past, s_delta, s_conv, s_hgrn, prm, tiles=None):
    B, T, _ = x.shape
    tm, nc, tq = tiles or _tiles(T)
    past = 0 if ckv_past is None else ckv_past.shape[2]
    tabs = _rope_tables(past + jnp.arange(T, dtype=jnp.int32))
    nd, ncv, nk, nr, nh = [], [], [], [], []
    for l in range(DEPTH):
        m = mod[l].reshape(B, 3, D)
        g_pre, g_post = prm["g_pre"][l].reshape(1, D), prm["g_post"][l].reshape(1, D)
        if l % 2 == 0:
            e = l // 2
            wts = prm["even"][e]
            qkv, za, zb, small, ckv, qp, kp, vp, kr_new = _even_in(x, m, g_pre, wts, tabs, tm)
            o_a, s_new, conv_new = _gdn(qkv, small, za, prm["conv_w"][e], s_conv[e], s_delta[e], prm["a_log"][e],
                                        prm["dt_bias"][e], prm["g_norm_a"][e].reshape(1, DV_A), min(8, T // CHUNK))
            if past:
                assert past % CHUNK == 0 and T <= CHUNK
                o_b = _attn_decode(qp, ckv_past[e], kr_past[e], ckv, small, wts, zb)
            else:
                o_b = _attention(qp, kp, vp, zb, tq)
            w_out = prm["w_out_e"][e]
            pending = (o_a, o_b, w_out[:W_A], w_out[W_A:], m, g_post)
            nd.append(s_new)
            ncv.append(conv_new)
            nk.append(ckv)
            nr.append(kr_new)
        else:
            j = l // 2
            x, s_new = _odd_layer(x, pending, m, g_pre, g_post, prm["w_in_o"][j], prm["w_out_o"][j], prm["lb_logits"],
                                  s_hgrn[j], prm["g_norm_c"][j].reshape(1, DV_C), l, min(8, T // CHUNK))
            nh.append(s_new)
    return (x, jnp.stack(nd), jnp.stack(ncv), jnp.stack(nk), jnp.stack(nr), jnp.stack(nh))


def _forward(x_prompt, x_sample, c_prompt, c_sample, cache_ckv, cache_kr, state_delta, state_conv, state_hgrn,
             w_ada, b_ada, g_pre, g_post, w_in_e, conv_w, a_log, dt_bias, g_norm_a, g_q, w_uq, g_kv, w_ukv,
             w_out_e, w_in_o, lb_logits, g_norm_c, w_out_o, tiles_prompt=None, tiles_sample=None):
    bp, bs = x_prompt.shape[0], x_sample.shape[0]
    n_even, n_odd = w_in_e.shape[0], w_in_o.shape[0]
    mod = _ada(jnp.concatenate([c_prompt, c_sample], axis=0), w_ada, b_ada)
    prm = dict(
        g_pre=g_pre, g_post=g_post, conv_w=conv_w, a_log=a_log, dt_bias=dt_bias, g_norm_a=g_norm_a,
        g_norm_c=g_norm_c, lb_logits=lb_logits,
        even=[_even_weights(w_in_e[e], g_q[e], w_uq[e], g_kv[e], w_ukv[e]) for e in range(n_even)],
        w_out_e=w_out_e.astype(BF16), w_in_o=w_in_o.astype(BF16), w_out_o=w_out_o.astype(BF16))
    zeros = lambda *s: jnp.zeros(s, F32)
    outs_p = _trunk(x_prompt, mod[:, :bp], None, None, zeros(n_even, bp, H_A, DK_A, DV_A),
                    zeros(n_even, bp, CONV_W - 1, QKV_A), zeros(n_odd, bp, H_C, DK_C, DV_C), prm, tiles_prompt)
    outs_s = _trunk(x_sample, mod[:, bp:], cache_ckv, cache_kr, state_delta, state_conv, state_hgrn, prm,
                    tiles_sample)
    return (outs_p[0], outs_s[0]) + outs_p[1:] + outs_s[1:]


def kernel(x_prompt, x_sample, c_prompt, c_sample, cache_ckv, cache_kr, state_delta, state_conv, state_hgrn,
           w_ada, b_ada, g_pre, g_post, w_in_e, conv_w, a_log, dt_bias, g_norm_a, g_q, w_uq, g_kv, w_ukv,
           w_out_e, w_in_o, lb_logits, g_norm_c, w_out_o):
    return _forward(x_prompt, x_sample, c_prompt, c_sample, cache_ckv, cache_kr, state_delta, state_conv,
                    state_hgrn, w_ada, b_ada, g_pre, g_post, w_in_e, conv_w, a_log, dt_bias, g_norm_a, g_q, w_uq,
                    g_kv, w_ukv, w_out_e, w_in_o, lb_logits, g_norm_c, w_out_o)
```

```python
import functools
import math

import jax
import jax.numpy as jnp
from jax import lax
from jax.experimental import pallas as pl
from jax.experimental.pallas import tpu as pltpu

F32 = jnp.float32
BF16 = jnp.bfloat16

D = 1024
DEPTH = 4
CHUNK = 64
EPS = 1e-6
H_A, DK_A, DV_A, CONV_W = 4, 128, 128, 4
H_B, Q_LORA, KV_LORA, D_NOPE, D_ROPE, DV_B = 8, 384, 256, 64, 32, 64
ROPE_THETA = 10000.0
H_C, DK_C, DV_C = 8, 128, 128
W_A, W_B, W_C = H_A * DV_A, H_B * DV_B, H_C * DV_C
QKV_A = H_A * (2 * DK_A + DV_A)
LANES = 128
SUB = 16
KS_USED = SUB * (CHUNK // SUB) * (CHUNK // SUB + 1) // 2
KS_ROWS = 2 * LANES
EXP_CLAMP = 60.0
NEG_BIG = -1e30
VMEM_LIMIT = 56 * 1024 * 1024


def _cparams(n_axes):
    return pltpu.CompilerParams(dimension_semantics=("arbitrary",) * n_axes, vmem_limit_bytes=VMEM_LIMIT)


def _mm(a, b):
    return jnp.dot(a.astype(BF16), b.astype(BF16), preferred_element_type=F32)


def _mm_nt(a, b):
    return lax.dot_general(a.astype(BF16), b.astype(BF16), (((1,), (1,)), ((), ())), preferred_element_type=F32)


def _mm_tn(a, b):
    return lax.dot_general(a.astype(BF16), b.astype(BF16), (((0,), (0,)), ((), ())), preferred_element_type=F32)


def _split3(x):
    hi = x.astype(BF16)
    r1 = x - hi.astype(F32)
    mid = r1.astype(BF16)
    lo = (r1 - mid.astype(F32)).astype(BF16)
    return hi, mid, lo


def _cum_left(l01, x):
    hi, mid, lo = _split3(x)
    d = functools.partial(jnp.dot, preferred_element_type=F32)
    return d(l01, hi) + d(l01, mid) + d(l01, lo)


def _cum_right(x, r01):
    hi, mid, lo = _split3(x)
    d = functools.partial(jnp.dot, preferred_element_type=F32)
    return d(hi, r01) + d(mid, r01) + d(lo, r01)


def _sigmoid(x):
    return 1.0 / (1.0 + jnp.exp(-x))


def _silu(x):
    return x * _sigmoid(x)


def _softplus(x):
    return jnp.maximum(x, 0.0) + jnp.log(1.0 + jnp.exp(-jnp.abs(x)))


def _rms(x, g):
    return x * lax.rsqrt(jnp.mean(x * x, axis=-1, keepdims=True) + EPS) * g


def _prenorm(x, mod_ref, g_ref):
    return _rms(x, g_ref[...]) * (1.0 + mod_ref[0, 1:2, :]) + mod_ref[0, 0:1, :]


def _iota2(shape):
    return lax.broadcasted_iota(jnp.int32, shape, 0), lax.broadcasted_iota(jnp.int32, shape, 1)


def _ada_kernel(c_ref, w_ref, b_ref, o_ref):
    o_ref[0] = _mm(_silu(c_ref[...]), w_ref[0]) + b_ref[0]


def _ada(c, w_ada, b_ada):
    nb = c.shape[0]
    tn = 1024
    return pl.pallas_call(
        _ada_kernel,
        grid=(DEPTH, 3 * D // tn),
        in_specs=[pl.BlockSpec((nb, D), lambda l, j: (0, 0)),
                  pl.BlockSpec((1, D, tn), lambda l, j: (l, 0, j)),
                  pl.BlockSpec((1, 1, tn), lambda l, j: (l, 0, j))],
        out_specs=pl.BlockSpec((1, nb, tn), lambda l, j: (l, 0, j)),
        out_shape=jax.ShapeDtypeStruct((DEPTH, nb, 3 * D), F32),
        compiler_params=_cparams(2),
        name="ada_mod",
    )(c, w_ada, b_ada.reshape(DEPTH, 1, 3 * D))


NB_E = QKV_A + W_A + Q_LORA + KV_LORA + W_B
KR_LANE = 0
GATE_LANE = KR_LANE + D_ROPE


def _value_ones(n):
    lane = lax.broadcasted_iota(jnp.int32, (1, n), 1)
    return ((lane & (LANES - 1)) >= DV_B).astype(F32)


def _even_in_kernel(x_ref, mod_ref, gpre_ref, wbig_ref, wsm_ref, gq_ref, wq_ref, gkv_ref, wkv_ref,
                    cq_ref, sq_ref, ck_ref, sk_ref,
                    qkv_ref, za_ref, zb_ref, small_ref, ckv_ref, qp_ref, kp_ref, vp_ref, kr_ref):
    hb = _prenorm(x_ref[0], mod_ref, gpre_ref).astype(BF16)
    big = jnp.dot(hb, wbig_ref[...], preferred_element_type=F32)
    o = 0
    qkv_ref[0] = big[:, o:o + QKV_A]
    o += QKV_A
    za_ref[0] = big[:, o:o + W_A]
    o += W_A
    cq = big[:, o:o + Q_LORA]
    o += Q_LORA
    ckv = big[:, o:o + KV_LORA]
    o += KV_LORA
    zb_ref[0] = big[:, o:o + W_B]
    half = D_ROPE // 2

    def rotate_pairs(v, first):
        n = v.shape[1]
        lane = lax.broadcasted_iota(jnp.int32, v.shape, 1) & (LANES - 1)
        return jnp.where(lane < first + half, pltpu.roll(v, n - half, 1), pltpu.roll(v, half, 1))

    sm = jnp.dot(hb, wsm_ref[...], preferred_element_type=F32)
    small = sm * ck_ref[...] + rotate_pairs(sm, KR_LANE) * sk_ref[...]
    small_ref[0] = small
    kr_ref[0] = small[:, KR_LANE:KR_LANE + D_ROPE]
    q1 = _mm(_rms(cq, gq_ref[...]), wq_ref[...])
    nq = H_B * LANES
    cq_t = jnp.concatenate([cq_ref[...]] * H_B, axis=1)
    sq_t = jnp.concatenate([sq_ref[...]] * H_B, axis=1)
    qp_ref[0] = (q1 * cq_t + rotate_pairs(q1, D_NOPE) * sq_t).astype(BF16)
    cn = _rms(ckv, gkv_ref[...])
    ckv_ref[0] = cn
    kv2 = _mm(cn, wkv_ref[...])
    lane = lax.broadcasted_iota(jnp.int32, small.shape, 1)
    kpe = jnp.where((lane >= D_NOPE) & (lane < D_NOPE + D_ROPE), pltpu.roll(small, D_NOPE - KR_LANE, 1), 0.0)
    kp_ref[0] = (kv2[:, :nq] + jnp.concatenate([kpe] * H_B, axis=1)).astype(BF16)
    vp_ref[0] = (kv2[:, nq:] + _value_ones(nq)).astype(BF16)


def _even_in(x, mod, g_pre, wts, tabs, tm):
    B, T, _ = x.shape
    grid = (B, T // tm)
    full = lambda a: pl.BlockSpec(a.shape, lambda b, i: (0,) * a.ndim)
    tok = lambda n: pl.BlockSpec((1, tm, n), lambda b, i: (b, i, 0))
    tab = pl.BlockSpec((tm, LANES), lambda b, i: (i, 0))
    outs = [(QKV_A, F32), (W_A, F32), (W_B, F32), (LANES, F32), (KV_LORA, F32),
            (H_B * LANES, BF16), (H_B * LANES, BF16), (H_B * LANES, BF16), (D_ROPE, F32)]
    return pl.pallas_call(
        _even_in_kernel,
        grid=grid,
        in_specs=[tok(D), pl.BlockSpec((1, 3, D), lambda b, i: (b, 0, 0)), full(g_pre),
                  full(wts["wbig"]), full(wts["wsm"]), full(wts["gq"]), full(wts["wq"]), full(wts["gkv"]),
                  full(wts["wkv"]), tab, tab, tab, tab],
        out_specs=[tok(n) for n, _ in outs],
        out_shape=[jax.ShapeDtypeStruct((B, T, n), dt) for n, dt in outs],
        compiler_params=_cparams(2),
        name="even_in",
    )(x, mod, g_pre, wts["wbig"], wts["wsm"], wts["gq"], wts["wq"], wts["gkv"], wts["wkv"], *tabs)


def _unit_lower_inverse(ms, r, c):
    eye = (r == c).astype(F32)
    same16 = (r >> 4) == (c >> 4)
    same32 = (r >> 5) == (c >> 5)
    off16 = same32 & jnp.logical_not(same16)
    md = [jnp.where(same16, m, 0.0) for m in ms]
    m1 = [jnp.where(off16, m, 0.0).astype(BF16) for m in ms]
    m2 = [jnp.where(same32, 0.0, m).astype(BF16) for m in ms]
    p = [eye - x for x in md]
    q = [_mm(x, x) for x in md]
    for step in range(3):
        p = [a + _mm(a, b) for a, b in zip(p, q)]
        if step < 2:
            q = [_mm(b, b) for b in q]
    for mk in (m1, m2):
        t = [_mm(a, b) for a, b in zip(p, mk)]
        p = [a - _mm(b, a) for a, b in zip(p, t)]
    return p


def _gdn_kernel(nc, qkv_ref, sm_ref, smt_ref, za_ref, convw_ref, cbuf_ref, s0_ref, rowc_ref, colc_ref, gn_ref,
                o_ref, sout_ref, cout_ref, xpad, act, s_scr):
    tc = nc * CHUNK

    @pl.when(pl.program_id(1) == 0)
    def _():
        xpad[0:8, :] = cbuf_ref[0]
        s_scr[...] = s0_ref[0]

    xpad[8:8 + tc, :] = qkv_ref[0]
    w = convw_ref[...]
    y = (w[0:1] * xpad[5:5 + tc, :] + w[1:2] * xpad[6:6 + tc, :] + w[2:3] * xpad[7:7 + tc, :]
         + w[3:4] * xpad[8:8 + tc, :])
    act[...] = _silu(y)
    tail = xpad[tc:tc + 8, :]
    cout_ref[0] = tail
    xpad[0:8, :] = tail

    r, c = _iota2((CHUNK, CHUNK))
    tril = c <= r
    strict = c < r
    tril_b = tril.astype(F32).astype(BF16)
    triu_b = (r <= c).astype(F32).astype(BF16)
    dtb_row, alog_row = rowc_ref[0:1, :], rowc_ref[1:2, :]
    dtb_col, alog_col = colc_ref[0, :, 0:CHUNK], colc_ref[1, :, 0:CHUNK]

    items = [(ci, h) for ci in range(nc) for h in range(H_A)]
    rows = [slice(ci * CHUNK, (ci + 1) * CHUNK) for ci in range(nc)]
    sm = sm_ref[0]
    beta_all = _sigmoid(sm)
    g_all = -jnp.exp(alog_row) * _softplus(sm + dtb_row)
    gc_all = [_cum_left(tril_b, g_all[rows[ci]]) for ci in range(nc)]
    gct = [_cum_right(-jnp.exp(alog_col) * _softplus(smt_ref[0, ci] + dtb_col), triu_b) for ci in range(nc)]

    def l2n(x):
        return x * lax.rsqrt(jnp.sum(x * x, axis=-1, keepdims=True) + EPS)

    qs = [l2n(act[rows[ci], h * DK_A:(h + 1) * DK_A]) * DK_A ** -0.5 for ci, h in items]
    ks = [l2n(act[rows[ci], (H_A + h) * DK_A:(H_A + h + 1) * DK_A]) for ci, h in items]
    vs = [act[rows[ci], 2 * H_A * DK_A + h * DV_A:2 * H_A * DK_A + (h + 1) * DV_A] for ci, h in items]
    bcol = [beta_all[rows[ci], GATE_LANE + h:GATE_LANE + h + 1] for ci, h in items]
    gcol = [gc_all[ci][:, GATE_LANE + H_A + h:GATE_LANE + H_A + h + 1] for ci, h in items]
    dm = [jnp.exp(jnp.where(tril, gcol[n] - gct[ci][H_A + h:H_A + h + 1, :], NEG_BIG))
          for n, (ci, h) in enumerate(items)]
    kb = [k * b for k, b in zip(ks, bcol)]
    kbf = [k.astype(BF16) for k in ks]
    ms = [_mm_nt(a, b) * jnp.where(strict, d, 0.0) for a, b, d in zip(kb, kbf, dm)]
    tinv = _unit_lower_inverse(ms, r, c)
    egc = [jnp.exp(g) for g in gcol]
    uw = [_mm(t, jnp.concatenate([v * b, x * e], axis=1)) for t, v, b, x, e in zip(tinv, vs, bcol, kb, egc)]
    attn = [(_mm_nt(q, k) * d).astype(BF16) for q, k, d in zip(qs, kbf, dm)]
    gl = [g[CHUNK - 1:CHUNK, :] for g in gcol]
    wq = [jnp.concatenate([x[:, DV_A:], q * e], axis=0).astype(BF16) for x, q, e in zip(uw, qs, egc)]
    kgt = [jnp.transpose(k * jnp.exp(l - g)).astype(BF16) for k, l, g in zip(ks, gl, gcol)]
    egl = [jnp.exp(l) for l in gl]

    s = [s_scr[h] for h in range(H_A)]
    for ci in range(nc):
        idx = [ci * H_A + h for h in range(H_A)]
        ws = [_mm(wq[n], s[h]) for h, n in enumerate(idx)]
        v_new = [uw[n][:, :DV_A] - ws[h][:CHUNK] for h, n in enumerate(idx)]
        o = [ws[h][CHUNK:] + _mm(attn[n], v_new[h]) for h, n in enumerate(idx)]
        s = [s[h] * egl[n] + _mm(kgt[n], v_new[h]) for h, n in enumerate(idx)]
        for h in range(H_A):
            lanes = slice(h * DV_A, (h + 1) * DV_A)
            gated = _rms(o[h], gn_ref[...]) * _silu(za_ref[0, rows[ci], lanes])
            o_ref[0, rows[ci], lanes] = gated.astype(BF16)
    for h in range(H_A):
        s_scr[h] = s[h]
    sout_ref[0] = s_scr[...]


def _gdn(qkv, small, za, conv_w, conv_buf, s_delta, a_log, dt_bias, g_norm_a, nc):
    B, T, _ = qkv.shape
    tc = nc * CHUNK
    gates = small[..., GATE_LANE:GATE_LANE + 2 * H_A]
    smt = gates.reshape(B, T // CHUNK, CHUNK, 2 * H_A).transpose(0, 1, 3, 2)
    lane = jnp.zeros((LANES,), F32)
    a_lanes = slice(GATE_LANE + H_A, GATE_LANE + 2 * H_A)
    rowc = jnp.stack([lane.at[a_lanes].set(dt_bias), lane.at[a_lanes].set(a_log)])
    col = jnp.zeros((8,), F32)
    colc = jnp.stack([col.at[H_A:2 * H_A].set(dt_bias), col.at[H_A:2 * H_A].set(a_log)])
    colc = jnp.broadcast_to(colc[:, :, None], (2, 8, LANES))
    cbuf = jnp.pad(conv_buf, ((0, 0), (8 - (CONV_W - 1), 0), (0, 0)))
    tok = lambda n: pl.BlockSpec((1, tc, n), lambda b, i: (b, i, 0))
    full = lambda a: pl.BlockSpec(a.shape, lambda b, i: (0,) * a.ndim)
    o, s_new, ctail = pl.pallas_call(
        functools.partial(_gdn_kernel, nc),
        grid=(B, T // tc),
        in_specs=[tok(QKV_A), tok(LANES), pl.BlockSpec((1, nc, 8, CHUNK), lambda b, i: (b, i, 0, 0)), tok(W_A),
                  full(conv_w), pl.BlockSpec((1, 8, QKV_A), lambda b, i: (b, 0, 0)),
                  pl.BlockSpec((1, H_A, DK_A, DV_A), lambda b, i: (b, 0, 0, 0)),
                  full(rowc), full(colc), full(g_norm_a)],
        out_specs=[tok(W_A), pl.BlockSpec((1, H_A, DK_A, DV_A), lambda b, i: (b, 0, 0, 0)),
                   pl.BlockSpec((1, 8, QKV_A), lambda b, i: (b, 0, 0))],
        out_shape=[jax.ShapeDtypeStruct((B, T, W_A), BF16), jax.ShapeDtypeStruct((B, H_A, DK_A, DV_A), F32),
                   jax.ShapeDtypeStruct((B, 8, QKV_A), F32)],
        scratch_shapes=[pltpu.VMEM((tc + 8, QKV_A), F32), pltpu.VMEM((tc, QKV_A), F32),
                        pltpu.VMEM((H_A, DK_A, DV_A), F32)],
        compiler_params=_cparams(2),
        name="gdn",
    )(qkv, small, smt, za, conv_w, cbuf, s_delta, rowc, colc, g_norm_a)
    return o, s_new, ctail[:, 8 - (CONV_W - 1):, :]


def _attn_kernel(tq, q_ref, k_ref, v_ref, zb_ref, o_ref, m_scr, acc_scr):
    i = pl.program_id(2)
    heads = (0, 1)
    hl = [slice(hh * LANES, (hh + 1) * LANES) for hh in heads]
    r, c = _iota2((tq, tq))
    mask = (c >> 6) <= (r >> 6)
    qs = [q_ref[0, :, hl[hh]] for hh in heads]
    m_scr[...] = jnp.full((2, tq, LANES), NEG_BIG, F32)
    acc_scr[...] = jnp.zeros((2, tq, LANES), F32)

    def tile(rows, tk, masked):
        s = [_mm_nt(qs[hh], k_ref[0, rows, hl[hh]]) for hh in heads]
        if masked:
            s = [jnp.where(mask, x, NEG_BIG) for x in s]
        m_prev = [m_scr[hh] for hh in heads]
        m_new = [jnp.maximum(m_prev[hh], jnp.max(s[hh], axis=-1, keepdims=True)) for hh in heads]
        alpha = [jnp.exp2(m_prev[hh] - m_new[hh]) for hh in heads]
        p = [jnp.exp2(s[hh] - jnp.concatenate([m_new[hh]] * (tk // LANES), axis=1)) for hh in heads]
        pv = [_mm(p[hh], v_ref[0, rows, hl[hh]]) for hh in heads]
        for hh in heads:
            acc_scr[hh] = alpha[hh] * acc_scr[hh] + pv[hh]
            m_scr[hh] = m_new[hh]

    def body(j, carry):
        tile(pl.ds(pl.multiple_of(j * 2 * tq, tq), 2 * tq), 2 * tq, False)
        return carry
    lax.fori_loop(0, i // 2, body, 0)

    @pl.when(i % 2 == 1)
    def _():
        tile(pl.ds(pl.multiple_of((i - 1) * tq, tq), tq), tq, False)

    tile(pl.ds(pl.multiple_of(i * tq, tq), tq), tq, True)
    _, lane = _iota2((tq, LANES))
    a0, a1 = acc_scr[0], acc_scr[1]
    o0 = a0 / pltpu.roll(a0, DV_B, 1)
    o1 = a1 / pltpu.roll(a1, DV_B, 1)
    o = jnp.where(lane < DV_B, o0, pltpu.roll(o1, DV_B, 1))
    o_ref[0] = (o * _silu(zb_ref[0])).astype(BF16)


def _attention(qp, kp, vp, zb, tq):
    B, T, _ = qp.shape
    pair = lambda rows, im: pl.BlockSpec((1, rows, 2 * LANES), im)
    return pl.pallas_call(
        functools.partial(_attn_kernel, tq),
        grid=(B, H_B // 2, T // tq),
        in_specs=[pair(tq, lambda b, h, i: (b, i, h)), pair(T, lambda b, h, i: (b, 0, h)),
                  pair(T, lambda b, h, i: (b, 0, h)), pl.BlockSpec((1, tq, LANES), lambda b, h, i: (b, i, h))],
        out_specs=pl.BlockSpec((1, tq, LANES), lambda b, h, i: (b, i, h)),
        out_shape=jax.ShapeDtypeStruct((B, T, W_B), BF16),
        scratch_shapes=[pltpu.VMEM((2, tq, LANES), F32)] * 2,
        compiler_params=_cparams(3),
        name="attn",
    )(qp, kp, vp, zb)


def _attn_decode_kernel(q_ref, cpast_ref, krpast_ref, cnew_ref, small_ref, wx_ref, wuv_ref, zb_ref, o_ref):
    t = q_ref.shape[1]
    q = q_ref[0]
    qx = jnp.concatenate([_mm(q[:, h * LANES:(h + 1) * LANES], wx_ref[h]) for h in range(H_B)], axis=0)
    ql, qr = qx[:, :KV_LORA].astype(BF16), qx[:, KV_LORA:].astype(BF16)
    segs = [(cpast_ref[0].astype(BF16), krpast_ref[0].astype(BF16)),
            (cnew_ref[0].astype(BF16), small_ref[0].astype(BF16))]
    s = [_mm_nt(ql, c) + _mm_nt(qr, kr) for c, kr in segs]
    m = jnp.maximum(jnp.max(s[0], axis=-1, keepdims=True), jnp.max(s[1], axis=-1, keepdims=True))
    p = [jnp.exp2(x - m) for x in s]
    l = jnp.sum(p[0], axis=-1, keepdims=True) + jnp.sum(p[1], axis=-1, keepdims=True)
    ol = (_mm(p[0], segs[0][0]) + _mm(p[1], segs[1][0])) / l
    o = _mm(ol[0:t], wuv_ref[0])
    for h in range(1, H_B):
        o = o + _mm(ol[h * t:(h + 1) * t], wuv_ref[h])
    o_ref[0] = (o * _silu(zb_ref[0])).astype(BF16)


def _attn_decode(qp, ckv_past, kr_past, ckv_new, small_new, wts, zb):
    B, T, _ = qp.shape
    P = ckv_past.shape[1]
    krp = jnp.pad(kr_past, ((0, 0), (0, 0), (KR_LANE, LANES - KR_LANE - D_ROPE)))
    row = lambda r, n: pl.BlockSpec((1, r, n), lambda b: (b, 0, 0))
    full = lambda a: pl.BlockSpec(a.shape, lambda b: (0,) * a.ndim)
    return pl.pallas_call(
        _attn_decode_kernel,
        grid=(B,),
        in_specs=[row(T, H_B * LANES), row(P, KV_LORA), row(P, LANES), row(T, KV_LORA), row(T, LANES),
                  full(wts["wx"]), full(wts["wuv"]), row(T, W_B)],
        out_specs=row(T, W_B),
        out_shape=jax.ShapeDtypeStruct((B, T, W_B), BF16),
        compiler_params=_cparams(1),
        name="attn_decode",
    )(qp, ckv_past, krp, ckv_new, small_new, wts["wx"], wts["wuv"], zb)


def _odd_layer_kernel(layer, nc, x_ref, oa_ref, ob_ref, wa_ref, wb_ref, modp_ref, gpostp_ref,
                      mod_ref, gpre_ref, w_ref, lbl_ref, s0_ref, gn_ref, wout_ref, gpost_ref,
                      xo_ref, sout_ref, qs_ref, lf_ref, iv_ref, z_ref, o_ref, s_scr, *bufs):
    @pl.when(pl.program_id(1) == 0)
    def _():
        s_scr[...] = s0_ref[0]

    mix = _mm(oa_ref[0], wa_ref[...]) + _mm(ob_ref[0], wb_ref[...])
    x = x_ref[0] + modp_ref[0, 2:3, :] * _rms(mix, gpostp_ref[...])
    hb = _prenorm(x, mod_ref, gpre_ref).astype(BF16)
    n = H_C * DK_C
    lg = lbl_ref[...]
    e = jnp.exp(lg - jnp.max(lg, axis=0, keepdims=True))
    p = e / jnp.sum(e, axis=0, keepdims=True)
    lb = jnp.sum(p[1:layer + 1], axis=0, keepdims=True)

    def project(rows, blocks):
        for blk in blocks:
            pr = jnp.dot(hb[rows], w_ref[:, blk * n:(blk + 1) * n], preferred_element_type=F32)
            if blk == 0:
                qs_ref[rows, :] = _silu(pr)
            elif blk == 1:
                lf_ref[rows, :] = jnp.log(lb + (1.0 - lb) * _sigmoid(pr))
            elif blk == 2:
                iv_ref[rows, :] = pr
            else:
                z_ref[rows, :] = pr

    nsub = CHUNK // SUB
    r2, c2 = _iota2((CHUNK, CHUNK))
    cum_l = (c2 <= r2).astype(F32).astype(BF16)
    rm, cm = _iota2((CHUNK, KS_ROWS))
    seg = jnp.zeros_like(cm)
    off = jnp.zeros_like(cm)
    for si in range(1, nsub):
        start = SUB * si * (si + 1) // 2
        seg = seg + (cm >= start).astype(jnp.int32)
        off = off + jnp.where(cm >= start, SUB * si, 0)
    pmask = ((rm >> 4) == seg) & ((cm - off) <= rm) & (cm < KS_USED)

    heads = range(H_C)
    hl = [slice(h * DK_C, (h + 1) * DK_C) for h in heads]

    def prep(ci, buf):
        qt_s, qg_s, kt_s, ks_s, it_s, dec_s = buf
        rows = slice(ci * CHUNK, (ci + 1) * CHUNK)
        lf = lf_ref[rows, :]
        qs = qs_ref[rows, :]
        ivb = iv_ref[rows, :].astype(BF16)
        g = _cum_left(cum_l, lf)
        wloc = jnp.concatenate([g[:SUB]] + [g[j * SUB:(j + 1) * SUB] - g[j * SUB - 1:j * SUB, :]
                                            for j in range(1, CHUNK // SUB)], axis=0)
        k = 1.0 - jnp.exp(lf)
        sub = [slice(j * SUB, (j + 1) * SUB) for j in range(nsub)]
        bound = [g[(j + 1) * SUB - 1:(j + 1) * SUB, :] for j in range(nsub)]
        diag = (k * jnp.exp(jnp.minimum(-wloc, EXP_CLAMP))).astype(BF16)
        base = [k[sub[j]] * jnp.exp(wloc[(j + 1) * SUB - 1:(j + 1) * SUB, :] - wloc[sub[j]]) for j in range(nsub - 1)]
        pieces = []
        for si in range(nsub):
            for j in range(si):
                far = base[j] if j + 1 == si else base[j] * jnp.exp(bound[si - 1] - bound[j])
                pieces.append(far.astype(BF16))
            pieces.append(diag[sub[si]])
        pieces.append(jnp.zeros((KS_ROWS - KS_USED, W_C), BF16))
        ks_s[...] = jnp.concatenate(pieces, axis=0)
        qt_s[...] = (qs * jnp.exp(wloc)).astype(BF16)
        qg_s[...] = (qs * jnp.exp(g)).astype(BF16)
        gl = g[CHUNK - 1:CHUNK, :]
        kt_s[...] = (k * jnp.exp(gl - g)).astype(BF16)
        dec_s[...] = jnp.broadcast_to(jnp.exp(gl), dec_s.shape)
        it_s[...] = jnp.concatenate([ivb[:(si + 1) * SUB] for si in range(nsub)]
                                    + [jnp.zeros((KS_ROWS - KS_USED, W_C), BF16)], axis=0)

    def mat(ci, buf):
        qt_s, qg_s, kt_s, ks_s, it_s, dec_s = buf
        rows = slice(ci * CHUNK, (ci + 1) * CHUNK)
        last = slice(KS_USED - CHUNK, KS_USED)
        st = [s_scr[h] for h in heads]
        sc = [_mm_nt(qt_s[:, hl[h]], ks_s[:, hl[h]]) for h in heads]
        upd = [_mm_tn(it_s[last, hl[h]], kt_s[:, hl[h]]) for h in heads]
        inter = [_mm_nt(qg_s[:, hl[h]], st[h]) for h in heads]
        p = [jnp.where(pmask, x, 0.0).astype(BF16) for x in sc]
        o = [_mm(p[h], it_s[:, hl[h]]) + inter[h] for h in heads]
        for h in heads:
            s_scr[h] = st[h] * dec_s[0:1, hl[h]] + upd[h]
            o_ref[rows, hl[h]] = _rms(o[h], gn_ref[...]) * _silu(z_ref[rows, hl[h]])

    buf_a, buf_b = bufs[:6], bufs[6:]
    tc = nc * CHUNK
    if nc == 1:
        project(slice(0, tc), range(4))
        prep(0, buf_a)
        mat(0, buf_a)
    else:
        npair = nc // 2
        project(slice(0, tc // 2), range(4))
        prep(0, buf_a)
        for i in range(npair):
            if i < npair // 2:
                per = 4 // (npair // 2)
                project(slice(tc // 2, tc), range(i * per, (i + 1) * per))
            c0 = 2 * i
            prep(c0 + 1, buf_b)
            mat(c0, buf_a)
            if c0 + 2 < nc:
                prep(c0 + 2, buf_a)
            mat(c0 + 1, buf_b)
    sout_ref[0] = s_scr[...]
    acc = _mm(o_ref[...], wout_ref[...])
    xo_ref[0] = x + mod_ref[0, 2:3, :] * _rms(acc, gpost_ref[...])


def _odd_layer(x, prev, mod, g_pre, g_post, w_in, w_out, lb_logits, s_hgrn, g_norm_c, layer, nc):
    oa, ob, wa, wb, modp, gpostp = prev
    B, T, _ = x.shape
    assert nc == 1 or nc % 4 == 0
    tc = nc * CHUNK
    operand_bufs = [pltpu.VMEM((CHUNK, W_C), BF16)] * 3 + [pltpu.VMEM((KS_ROWS, W_C), BF16)] * 2 + [
        pltpu.VMEM((8, W_C), F32)]
    tok = pl.BlockSpec((1, tc, D), lambda b, i: (b, i, 0))
    half = pl.BlockSpec((1, tc, W_A), lambda b, i: (b, i, 0))
    modspec = pl.BlockSpec((1, 3, D), lambda b, i: (b, 0, 0))
    st = pl.BlockSpec((1, H_C, DV_C, DK_C), lambda b, i: (b, 0, 0, 0))
    full = lambda a: pl.BlockSpec(a.shape, lambda b, i: (0,) * a.ndim)
    x_new, s_new = pl.pallas_call(
        functools.partial(_odd_layer_kernel, layer, nc),
        grid=(B, T // tc),
        in_specs=[tok, half, half, full(wa), full(wb), modspec, full(gpostp),
                  modspec, full(g_pre), full(w_in), full(lb_logits), st, full(g_norm_c), full(w_out), full(g_post)],
        out_specs=[tok, st],
        out_shape=[jax.ShapeDtypeStruct((B, T, D), F32), jax.ShapeDtypeStruct((B, H_C, DV_C, DK_C), F32)],
        scratch_shapes=[pltpu.VMEM((tc, W_C), F32)] * 5 + [pltpu.VMEM((H_C, DV_C, DK_C), F32)] + operand_bufs * 2,
        compiler_params=_cparams(2),
        name="odd_layer",
    )(x, oa, ob, wa, wb, modp, gpostp, mod, g_pre, w_in, lb_logits, jnp.swapaxes(s_hgrn, -1, -2), g_norm_c, w_out,
      g_post)
    return x_new, jnp.swapaxes(s_new, -1, -2)


def _even_weights(w_in, g_q, w_uq, g_kv, w_ukv):
    o_b, o_a, o_za = QKV_A, QKV_A + H_A, QKV_A + 2 * H_A
    o_cq = o_za + W_A
    o_ckv = o_cq + Q_LORA
    o_kr = o_ckv + KV_LORA
    o_zb = o_kr + D_ROPE
    half = D_ROPE // 2
    wbig = jnp.concatenate([w_in[:, :QKV_A], w_in[:, o_za:o_cq], w_in[:, o_cq:o_ckv], w_in[:, o_ckv:o_kr],
                            w_in[:, o_zb:]], axis=1)
    kr = w_in[:, o_kr:o_zb]
    z = lambda n: jnp.zeros((D, n), F32)
    wsm = jnp.concatenate([z(KR_LANE), kr, w_in[:, o_b:o_za], z(LANES - GATE_LANE - 2 * H_A)], axis=1)
    w3 = w_uq.reshape(Q_LORA, H_B, D_NOPE + D_ROPE)
    wq = jnp.concatenate([w3, jnp.zeros((Q_LORA, H_B, LANES - D_NOPE - D_ROPE), F32)], -1)
    wq = wq.reshape(Q_LORA, H_B * LANES)
    k3 = w_ukv.reshape(KV_LORA, H_B, D_NOPE + DV_B)
    wk = jnp.concatenate([k3[..., :D_NOPE], jnp.zeros((KV_LORA, H_B, LANES - D_NOPE), F32)], -1)
    wv = jnp.concatenate([k3[..., D_NOPE:], jnp.zeros((KV_LORA, H_B, LANES - DV_B), F32)], -1)
    wkv = jnp.concatenate([wk.reshape(KV_LORA, H_B * LANES), wv.reshape(KV_LORA, H_B * LANES)], axis=1)
    j = jnp.arange(D_ROPE)
    wx = jnp.zeros((H_B, LANES, KV_LORA + LANES), F32)
    wx = wx.at[:, :D_NOPE, :KV_LORA].set(jnp.transpose(k3[..., :D_NOPE], (1, 2, 0)))
    wx = wx.at[:, D_NOPE + j, KV_LORA + KR_LANE + j].set(1.0)
    hh = jnp.arange(H_B)
    wuv = jnp.zeros((H_B, KV_LORA, H_B, DV_B), F32).at[hh, :, hh, :].set(jnp.transpose(k3[..., D_NOPE:], (1, 0, 2)))
    wuv = wuv.reshape(H_B, KV_LORA, W_B)
    return dict(wbig=wbig.astype(BF16), wsm=wsm.astype(BF16), wq=wq.astype(BF16), wkv=wkv.astype(BF16),
                gq=g_q.reshape(1, Q_LORA), gkv=g_kv.reshape(1, KV_LORA),
                wx=wx.astype(BF16), wuv=wuv.astype(BF16))


def _rope_tables(pos):
    half = D_ROPE // 2
    inv = ROPE_THETA ** (-jnp.arange(half, dtype=F32) / half)
    ang = pos.astype(F32)[:, None] * inv[None, :]
    cos, sin = jnp.cos(ang), jnp.sin(ang)
    t = pos.shape[0]
    one, zero = jnp.ones, jnp.zeros
    scale = (D_NOPE + D_ROPE) ** -0.5 * math.log2(math.e)
    pad_q = LANES - D_NOPE - D_ROPE
    pad_k = LANES - KR_LANE - D_ROPE
    cq = scale * jnp.concatenate([one((t, D_NOPE), F32), cos, cos, zero((t, pad_q), F32)], axis=1)
    sq = scale * jnp.concatenate([zero((t, D_NOPE), F32), -sin, sin, zero((t, pad_q), F32)], axis=1)
    ck = jnp.concatenate([one((t, KR_LANE), F32), cos, cos, one((t, pad_k), F32)], axis=1)
    sk = jnp.concatenate([zero((t, KR_LANE), F32), -sin, sin, zero((t, pad_k), F32)], axis=1)
    return cq, sq, ck, sk


def _tiles(T):
    tm = min(256, T)
    nc = min(4, T // CHUNK)
    tq = min(512, T)
    return tm, nc, tq


def _trunk(x, mod, ckv_past, kr_past, s_delta, s_conv, s_hgrn, prm, tiles=None):
    B, T, _ = x.shape
    tm, nc, tq = tiles or _tiles(T)
    past = 0 if ckv_past is None else ckv_past.shape[2]
    tabs = _rope_tables(past + jnp.arange(T, dtype=jnp.int32))
    nd, ncv, nk, nr, nh = [], [], [], [], []
    for l in range(DEPTH):
        m = mod[l].reshape(B, 3, D)
        g_pre, g_post = prm["g_pre"][l].reshape(1, D), prm["g_post"][l].reshape(1, D)
        if l % 2 == 0:
            e = l // 2
            wts = prm["even"][e]
            qkv, za, zb, small, ckv, qp, kp, vp, kr_new = _even_in(x, m, g_pre, wts, tabs, tm)
            o_a, s_new, conv_new = _gdn(qkv, small, za, prm["conv_w"][e], s_conv[e], s_delta[e], prm["a_log"][e],
                                        prm["dt_bias"][e], prm["g_norm_a"][e].reshape(1, DV_A), min(8, T // CHUNK))
            if past:
                assert past % CHUNK == 0 and T <= CHUNK
                o_b = _attn_decode(qp, ckv_past[e], kr_past[e], ckv, small, wts, zb)
            else:
                o_b = _attention(qp, kp, vp, zb, tq)
            w_out = prm["w_out_e"][e]
            pending = (o_a, o_b, w_out[:W_A], w_out[W_A:], m, g_post)
            nd.append(s_new)
            ncv.append(conv_new)
            nk.append(ckv)
            nr.append(kr_new)
        else:
            j = l // 2
            x, s_new = _odd_layer(x, pending, m, g_pre, g_post, prm["w_in_o"][j], prm["w_out_o"][j], prm["lb_logits"],
                                  s_hgrn[j], prm["g_norm_c"][j].reshape(1, DV_C), l, min(8, T // CHUNK))
            nh.append(s_new)
    return (x, jnp.stack(nd), jnp.stack(ncv), jnp.stack(nk), jnp.stack(nr), jnp.stack(nh))


def _forward(x_prompt, x_sample, c_prompt, c_sample, cache_ckv, cache_kr, state_delta, state_conv, state_hgrn,
             w_ada, b_ada, g_pre, g_post, w_in_e, conv_w, a_log, dt_bias, g_norm_a, g_q, w_uq, g_kv, w_ukv,
             w_out_e, w_in_o, lb_logits, g_norm_c, w_out_o, tiles_prompt=None, tiles_sample=None):
    bp, bs = x_prompt.shape[0], x_sample.shape[0]
    n_even, n_odd = w_in_e.shape[0], w_in_o.shape[0]
    mod = _ada(jnp.concatenate([c_prompt, c_sample], axis=0), w_ada, b_ada)
    prm = dict(
        g_pre=g_pre, g_post=g_post, conv_w=conv_w, a_log=a_log, dt_bias=dt_bias, g_norm_a=g_norm_a,
        g_norm_c=g_norm_c, lb_logits=lb_logits,
        even=[_even_weights(w_in_e[e], g_q[e], w_uq[e], g_kv[e], w_ukv[e]) for e in range(n_even)],
        w_out_e=w_out_e.astype(BF16), w_in_o=w_in_o.astype(BF16), w_out_o=w_out_o.astype(BF16))
    zeros = lambda *s: jnp.zeros(s, F32)
    outs_p = _trunk(x_prompt, mod[:, :bp], None, None, zeros(n_even, bp, H_A, DK_A, DV_A),
                    zeros(n_even, bp, CONV_W - 1, QKV_A), zeros(n_odd, bp, H_C, DK_C, DV_C), prm, tiles_prompt)
    outs_s = _trunk(x_sample, mod[:, bp:], cache_ckv, cache_kr, state_delta, state_conv, state_hgrn, prm,
                    tiles_sample)
    return (outs_p[0], outs_s[0]) + outs_p[1:] + outs_s[1:]


def kernel(x_prompt, x_sample, c_prompt, c_sample, cache_ckv, cache_kr, state_delta, state_conv, state_hgrn,
           w_ada, b_ada, g_pre, g_post, w_in_e, conv_w, a_log, dt_bias, g_norm_a, g_q, w_uq, g_kv, w_ukv,
           w_out_e, w_in_o, lb_logits, g_norm_c, w_out_o):
    return _forward(x_prompt, x_sample, c_prompt, c_sample, cache_ckv, cache_kr, state_delta, state_conv,
                    state_hgrn, w_ada, b_ada, g_pre, g_post, w_in_e, conv_w, a_log, dt_bias, g_norm_a, g_q, w_uq,
                    g_kv, w_ukv, w_out_e, w_in_o, lb_logits, g_norm_c, w_out_o)
```

```python
import functools
import math

import jax
import jax.numpy as jnp
from jax import lax
from jax.experimental import pallas as pl
from jax.experimental.pallas import tpu as pltpu

F32 = jnp.float32
BF16 = jnp.bfloat16

D = 1024
DEPTH = 4
CHUNK = 64
EPS = 1e-6
H_A, DK_A, DV_A, CONV_W = 4, 128, 128, 4
H_B, Q_LORA, KV_LORA, D_NOPE, D_ROPE, DV_B = 8, 384, 256, 64, 32, 64
ROPE_THETA = 10000.0
H_C, DK_C, DV_C = 8, 128, 128
W_A, W_B, W_C = H_A * DV_A, H_B * DV_B, H_C * DV_C
QKV_A = H_A * (2 * DK_A + DV_A)
LANES = 128
SUB = 16
KS_USED = SUB * (CHUNK // SUB) * (CHUNK // SUB + 1) // 2
KS_ROWS = 2 * LANES
EXP_CLAMP = 60.0
NEG_BIG = -1e30
VMEM_LIMIT = 56 * 1024 * 1024


def _cparams(n_axes):
    return pltpu.CompilerParams(dimension_semantics=("arbitrary",) * n_axes, vmem_limit_bytes=VMEM_LIMIT)


def _mm(a, b):
    return jnp.dot(a.astype(BF16), b.astype(BF16), preferred_element_type=F32)


def _mm_nt(a, b):
    return lax.dot_general(a.astype(BF16), b.astype(BF16), (((1,), (1,)), ((), ())), preferred_element_type=F32)


def _mm_tn(a, b):
    return lax.dot_general(a.astype(BF16), b.astype(BF16), (((0,), (0,)), ((), ())), preferred_element_type=F32)


def _split3(x):
    hi = x.astype(BF16)
    r1 = x - hi.astype(F32)
    mid = r1.astype(BF16)
    lo = (r1 - mid.astype(F32)).astype(BF16)
    return hi, mid, lo


def _cum_left(l01, x):
    hi, mid, lo = _split3(x)
    d = functools.partial(jnp.dot, preferred_element_type=F32)
    return d(l01, hi) + d(l01, mid) + d(l01, lo)


def _sigmoid(x):
    return 1.0 / (1.0 + jnp.exp(-x))


def _silu(x):
    return x * _sigmoid(x)


def _softplus(x):
    return jnp.maximum(x, 0.0) + jnp.log(1.0 + jnp.exp(-jnp.abs(x)))


def _rms(x, g):
    return x * lax.rsqrt(jnp.mean(x * x, axis=-1, keepdims=True) + EPS) * g


def _prenorm(x, mod_ref, g_ref):
    return _rms(x, g_ref[...]) * (1.0 + mod_ref[0, 1:2, :]) + mod_ref[0, 0:1, :]


def _iota2(shape):
    return lax.broadcasted_iota(jnp.int32, shape, 0), lax.broadcasted_iota(jnp.int32, shape, 1)


def _ada_kernel(c_ref, w_ref, b_ref, o_ref):
    o_ref[0] = _mm(_silu(c_ref[...]), w_ref[0]) + b_ref[0]


def _ada(c, w_ada, b_ada):
    nb = c.shape[0]
    tn = 1024
    return pl.pallas_call(
        _ada_kernel,
        grid=(DEPTH, 3 * D // tn),
        in_specs=[pl.BlockSpec((nb, D), lambda l, j: (0, 0)),
                  pl.BlockSpec((1, D, tn), lambda l, j: (l, 0, j)),
                  pl.BlockSpec((1, 1, tn), lambda l, j: (l, 0, j))],
        out_specs=pl.BlockSpec((1, nb, tn), lambda l, j: (l, 0, j)),
        out_shape=jax.ShapeDtypeStruct((DEPTH, nb, 3 * D), F32),
        compiler_params=_cparams(2),
        name="ada_mod",
    )(c, w_ada, b_ada.reshape(DEPTH, 1, 3 * D))


KR_LANE = 0
GATE_LANE = KR_LANE + D_ROPE


def _value_ones(n):
    lane = lax.broadcasted_iota(jnp.int32, (1, n), 1)
    return ((lane & (LANES - 1)) >= DV_B).astype(F32)


def _unit_lower_inverse(ms, r, c):
    eye = (r == c).astype(F32)
    same16 = (r >> 4) == (c >> 4)
    same32 = (r >> 5) == (c >> 5)
    off16 = same32 & jnp.logical_not(same16)
    md = [jnp.where(same16, m, 0.0) for m in ms]
    m1 = [jnp.where(off16, m, 0.0).astype(BF16) for m in ms]
    m2 = [jnp.where(same32, 0.0, m).astype(BF16) for m in ms]
    p = [eye - x for x in md]
    q = [_mm(x, x) for x in md]
    for step in range(3):
        p = [a + _mm(a, b) for a, b in zip(p, q)]
        if step < 2:
            q = [_mm(b, b) for b in q]
    for mk in (m1, m2):
        t = [_mm(a, b) for a, b in zip(p, mk)]
        p = [a - _mm(b, a) for a, b in zip(p, t)]
    return p


def _even_layer_kernel(nc, x_ref, mod_ref, gpre_ref, wbig_ref, wsm_ref, gq_ref, wq_ref, gkv_ref, wkv_ref,
                       cq_ref, sq_ref, ck_ref, sk_ref, convw_ref, cbuf_ref, s0_ref, rowc_ref, gn_ref,
                       zb_ref, small_ref, ckv_ref, qp_ref, kp_ref, vp_ref, kr_ref, o_ref, sout_ref, cout_ref,
                       xpad, act, za_scr, s_scr):
    tc = nc * CHUNK
    half_rows = [slice(0, tc // 2), slice(tc // 2, tc)] if nc > 1 else [slice(0, tc)]
    half_chunks = [range(0, nc // 2), range(nc // 2, nc)] if nc > 1 else [range(nc)]

    @pl.when(pl.program_id(1) == 0)
    def _():
        xpad[0:8, :] = cbuf_ref[0]
        s_scr[...] = s0_ref[0]

    hb = _prenorm(x_ref[0], mod_ref, gpre_ref).astype(BF16)
    half = D_ROPE // 2
    nq = H_B * LANES

    def rotate_pairs(v, first):
        n = v.shape[1]
        lane = lax.broadcasted_iota(jnp.int32, v.shape, 1) & (LANES - 1)
        return jnp.where(lane < first + half, pltpu.roll(v, n - half, 1), pltpu.roll(v, half, 1))

    def project(rows):
        n0 = rows.start
        big = jnp.dot(hb[rows], wbig_ref[...], preferred_element_type=F32)
        o = 0
        xpad[8 + n0:8 + rows.stop, :] = big[:, o:o + QKV_A]
        o += QKV_A
        za_scr[rows, :] = big[:, o:o + W_A]
        o += W_A
        cq = big[:, o:o + Q_LORA]
        o += Q_LORA
        ckv = big[:, o:o + KV_LORA]
        o += KV_LORA
        zb_ref[0, rows, :] = big[:, o:o + W_B]
        sm = jnp.dot(hb[rows], wsm_ref[...], preferred_element_type=F32)
        small = sm * ck_ref[rows, :] + rotate_pairs(sm, KR_LANE) * sk_ref[rows, :]
        small_ref[0, rows, :] = small
        kr_ref[0, rows, :] = small[:, KR_LANE:KR_LANE + D_ROPE]
        q1 = _mm(_rms(cq, gq_ref[...]), wq_ref[...])
        cq_t = jnp.concatenate([cq_ref[rows, :]] * H_B, axis=1)
        sq_t = jnp.concatenate([sq_ref[rows, :]] * H_B, axis=1)
        qp_ref[0, rows, :] = (q1 * cq_t + rotate_pairs(q1, D_NOPE) * sq_t).astype(BF16)
        cn = _rms(ckv, gkv_ref[...])
        ckv_ref[0, rows, :] = cn
        kv2 = _mm(cn, wkv_ref[...])
        lane = lax.broadcasted_iota(jnp.int32, small.shape, 1)
        kpe = jnp.where((lane >= D_NOPE) & (lane < D_NOPE + D_ROPE), pltpu.roll(small, D_NOPE - KR_LANE, 1), 0.0)
        kp_ref[0, rows, :] = (kv2[:, :nq] + jnp.concatenate([kpe] * H_B, axis=1)).astype(BF16)
        vp_ref[0, rows, :] = (kv2[:, nq:] + _value_ones(nq)).astype(BF16)

    w = convw_ref[...]

    def conv_act(rows):
        a, b = rows.start, rows.stop
        y = (w[0:1] * xpad[5 + a:5 + b, :] + w[1:2] * xpad[6 + a:6 + b, :] + w[2:3] * xpad[7 + a:7 + b, :]
             + w[3:4] * xpad[8 + a:8 + b, :])
        act[rows, :] = _silu(y)

    r, c = _iota2((CHUNK, CHUNK))
    tril = c <= r
    strict = c < r
    tril_b = tril.astype(F32).astype(BF16)
    dtb_row, alog_row = rowc_ref[0:1, :], rowc_ref[1:2, :]

    def l2n(v):
        return v * lax.rsqrt(jnp.sum(v * v, axis=-1, keepdims=True) + EPS)

    def delta(chunks, s):
        items = [(ci, h) for ci in chunks for h in range(H_A)]
        rows = {ci: slice(ci * CHUNK, (ci + 1) * CHUNK) for ci in chunks}
        gc_all, gct, beta = {}, {}, {}
        for ci in chunks:
            sm = small_ref[0, rows[ci], :]
            beta[ci] = _sigmoid(sm)
            gc_all[ci] = _cum_left(tril_b, -jnp.exp(alog_row) * _softplus(sm + dtb_row))
            gct[ci] = jnp.transpose(gc_all[ci])
        qs = [l2n(act[rows[ci], h * DK_A:(h + 1) * DK_A]) * DK_A ** -0.5 for ci, h in items]
        ks = [l2n(act[rows[ci], (H_A + h) * DK_A:(H_A + h + 1) * DK_A]) for ci, h in items]
        vs = [act[rows[ci], 2 * H_A * DK_A + h * DV_A:2 * H_A * DK_A + (h + 1) * DV_A] for ci, h in items]
        bcol = [beta[ci][:, GATE_LANE + h:GATE_LANE + h + 1] for ci, h in items]
        g_lane = GATE_LANE + H_A
        gcol = [gc_all[ci][:, g_lane + h:g_lane + h + 1] for ci, h in items]
        dm = [jnp.exp(jnp.where(tril, gcol[n] - gct[ci][g_lane + h:g_lane + h + 1, :], NEG_BIG))
              for n, (ci, h) in enumerate(items)]
        kb = [k * b for k, b in zip(ks, bcol)]
        kbf = [k.astype(BF16) for k in ks]
        ms = [_mm_nt(a, b) * jnp.where(strict, d, 0.0) for a, b, d in zip(kb, kbf, dm)]
        tinv = _unit_lower_inverse(ms, r, c)
        egc = [jnp.exp(g) for g in gcol]
        uw = [_mm(t, jnp.concatenate([v * b, x * e], axis=1)) for t, v, b, x, e in zip(tinv, vs, bcol, kb, egc)]
        attn = [(_mm_nt(q, k) * d).astype(BF16) for q, k, d in zip(qs, kbf, dm)]
        gl = [g[CHUNK - 1:CHUNK, :] for g in gcol]
        wq = [jnp.concatenate([x[:, DV_A:], q * e], axis=0).astype(BF16) for x, q, e in zip(uw, qs, egc)]
        kgt = [jnp.transpose(k * jnp.exp(l - g)).astype(BF16) for k, l, g in zip(ks, gl, gcol)]
        egl = [jnp.exp(l) for l in gl]
        for pos, ci in enumerate(chunks):
            idx = [pos * H_A + h for h in range(H_A)]
            ws = [_mm(wq[n], s[h]) for h, n in enumerate(idx)]
            v_new = [uw[n][:, :DV_A] - ws[h][:CHUNK] for h, n in enumerate(idx)]
            o = [ws[h][CHUNK:] + _mm(attn[n], v_new[h]) for h, n in enumerate(idx)]
            s = [s[h] * egl[n] + _mm(kgt[n], v_new[h]) for h, n in enumerate(idx)]
            for h in range(H_A):
                lanes = slice(h * DV_A, (h + 1) * DV_A)
                gated = _rms(o[h], gn_ref[...]) * _silu(za_scr[rows[ci], lanes])
                o_ref[0, rows[ci], lanes] = gated.astype(BF16)
        return s

    s = [s_scr[h] for h in range(H_A)]
    project(half_rows[0])
    for hi, rows in enumerate(half_rows):
        if hi + 1 < len(half_rows):
            project(half_rows[hi + 1])
        conv_act(rows)
        s = delta(half_chunks[hi], s)
    tail = xpad[tc:tc + 8, :]
    cout_ref[0] = tail
    xpad[0:8, :] = tail
    for h in range(H_A):
        s_scr[h] = s[h]
    sout_ref[0] = s_scr[...]


def _even_layer(x, mod, g_pre, wts, tabs, conv_w, conv_buf, s_delta, a_log, dt_bias, g_norm_a, nc):
    B, T, _ = x.shape
    assert nc == 1 or nc % 2 == 0
    tc = nc * CHUNK
    lane = jnp.zeros((LANES,), F32)
    a_lanes = slice(GATE_LANE + H_A, GATE_LANE + 2 * H_A)
    rowc = jnp.stack([lane.at[a_lanes].set(dt_bias), lane.at[a_lanes].set(a_log)])
    cbuf = jnp.pad(conv_buf, ((0, 0), (8 - (CONV_W - 1), 0), (0, 0)))
    full = lambda a: pl.BlockSpec(a.shape, lambda b, i: (0,) * a.ndim)
    tok = lambda n: pl.BlockSpec((1, tc, n), lambda b, i: (b, i, 0))
    tab = pl.BlockSpec((tc, LANES), lambda b, i: (i, 0))
    state = pl.BlockSpec((1, H_A, DK_A, DV_A), lambda b, i: (b, 0, 0, 0))
    tail = pl.BlockSpec((1, 8, QKV_A), lambda b, i: (b, 0, 0))
    outs = [(W_B, F32), (LANES, F32), (KV_LORA, F32), (H_B * LANES, BF16), (H_B * LANES, BF16), (H_B * LANES, BF16),
            (D_ROPE, F32), (W_A, BF16)]
    res = pl.pallas_call(
        functools.partial(_even_layer_kernel, nc),
        grid=(B, T // tc),
        in_specs=[tok(D), pl.BlockSpec((1, 3, D), lambda b, i: (b, 0, 0)), full(g_pre),
                  full(wts["wbig"]), full(wts["wsm"]), full(wts["gq"]), full(wts["wq"]), full(wts["gkv"]),
                  full(wts["wkv"]), tab, tab, tab, tab, full(conv_w), tail, state, full(rowc), full(g_norm_a)],
        out_specs=[tok(n) for n, _ in outs] + [state, tail],
        out_shape=[jax.ShapeDtypeStruct((B, T, n), dt) for n, dt in outs]
        + [jax.ShapeDtypeStruct((B, H_A, DK_A, DV_A), F32), jax.ShapeDtypeStruct((B, 8, QKV_A), F32)],
        scratch_shapes=[pltpu.VMEM((tc + 8, QKV_A), F32), pltpu.VMEM((tc, QKV_A), F32), pltpu.VMEM((tc, W_A), F32),
                        pltpu.VMEM((H_A, DK_A, DV_A), F32)],
        compiler_params=_cparams(2),
        name="even_layer",
    )(x, mod, g_pre, wts["wbig"], wts["wsm"], wts["gq"], wts["wq"], wts["gkv"], wts["wkv"], *tabs, conv_w, cbuf,
      s_delta, rowc, g_norm_a)
    zb, small, ckv, qp, kp, vp, kr_new, o_a, s_new, ctail = res
    return zb, small, ckv, qp, kp, vp, kr_new, o_a, s_new, ctail[:, 8 - (CONV_W - 1):, :]


def _attn_kernel(tq, q_ref, k_ref, v_ref, zb_ref, o_ref, m_scr, acc_scr):
    i = pl.program_id(2)
    heads = (0, 1)
    hl = [slice(hh * LANES, (hh + 1) * LANES) for hh in heads]
    r, c = _iota2((tq, tq))
    mask = (c >> 6) <= (r >> 6)
    qs = [q_ref[0, :, hl[hh]] for hh in heads]
    m_scr[...] = jnp.full((2, tq, LANES), NEG_BIG, F32)
    acc_scr[...] = jnp.zeros((2, tq, LANES), F32)

    def tile(rows, tk, masked):
        s = [_mm_nt(qs[hh], k_ref[0, rows, hl[hh]]) for hh in heads]
        if masked:
            s = [jnp.where(mask, x, NEG_BIG) for x in s]
        m_prev = [m_scr[hh] for hh in heads]
        m_new = [jnp.maximum(m_prev[hh], jnp.max(s[hh], axis=-1, keepdims=True)) for hh in heads]
        alpha = [jnp.exp2(m_prev[hh] - m_new[hh]) for hh in heads]
        p = [jnp.exp2(s[hh] - jnp.concatenate([m_new[hh]] * (tk // LANES), axis=1)) for hh in heads]
        pv = [_mm(p[hh], v_ref[0, rows, hl[hh]]) for hh in heads]
        for hh in heads:
            acc_scr[hh] = alpha[hh] * acc_scr[hh] + pv[hh]
            m_scr[hh] = m_new[hh]

    def body(j, carry):
        tile(pl.ds(pl.multiple_of(j * 2 * tq, tq), 2 * tq), 2 * tq, False)
        return carry
    lax.fori_loop(0, i // 2, body, 0)

    @pl.when(i % 2 == 1)
    def _():
        tile(pl.ds(pl.multiple_of((i - 1) * tq, tq), tq), tq, False)

    tile(pl.ds(pl.multiple_of(i * tq, tq), tq), tq, True)
    _, lane = _iota2((tq, LANES))
    a0, a1 = acc_scr[0], acc_scr[1]
    o0 = a0 / pltpu.roll(a0, DV_B, 1)
    o1 = a1 / pltpu.roll(a1, DV_B, 1)
    o = jnp.where(lane < DV_B, o0, pltpu.roll(o1, DV_B, 1))
    o_ref[0] = (o * _silu(zb_ref[0])).astype(BF16)


def _attention(qp, kp, vp, zb, tq):
    B, T, _ = qp.shape
    pair = lambda rows, im: pl.BlockSpec((1, rows, 2 * LANES), im)
    return pl.pallas_call(
        functools.partial(_attn_kernel, tq),
        grid=(B, H_B // 2, T // tq),
        in_specs=[pair(tq, lambda b, h, i: (b, i, h)), pair(T, lambda b, h, i: (b, 0, h)),
                  pair(T, lambda b, h, i: (b, 0, h)), pl.BlockSpec((1, tq, LANES), lambda b, h, i: (b, i, h))],
        out_specs=pl.BlockSpec((1, tq, LANES), lambda b, h, i: (b, i, h)),
        out_shape=jax.ShapeDtypeStruct((B, T, W_B), BF16),
        scratch_shapes=[pltpu.VMEM((2, tq, LANES), F32)] * 2,
        compiler_params=_cparams(3),
        name="attn",
    )(qp, kp, vp, zb)


def _attn_decode_kernel(q_ref, cpast_ref, krpast_ref, cnew_ref, small_ref, wx_ref, wuv_ref, zb_ref, o_ref):
    t = q_ref.shape[1]
    q = q_ref[0]
    qx = jnp.concatenate([_mm(q[:, h * LANES:(h + 1) * LANES], wx_ref[h]) for h in range(H_B)], axis=0)
    ql, qr = qx[:, :KV_LORA].astype(BF16), qx[:, KV_LORA:].astype(BF16)
    segs = [(cpast_ref[0].astype(BF16), krpast_ref[0].astype(BF16)),
            (cnew_ref[0].astype(BF16), small_ref[0].astype(BF16))]
    s = [_mm_nt(ql, c) + _mm_nt(qr, kr) for c, kr in segs]
    m = jnp.maximum(jnp.max(s[0], axis=-1, keepdims=True), jnp.max(s[1], axis=-1, keepdims=True))
    p = [jnp.exp2(x - m) for x in s]
    l = jnp.sum(p[0], axis=-1, keepdims=True) + jnp.sum(p[1], axis=-1, keepdims=True)
    ol = (_mm(p[0], segs[0][0]) + _mm(p[1], segs[1][0])) / l
    o = _mm(ol[0:t], wuv_ref[0])
    for h in range(1, H_B):
        o = o + _mm(ol[h * t:(h + 1) * t], wuv_ref[h])
    o_ref[0] = (o * _silu(zb_ref[0])).astype(BF16)


def _attn_decode(qp, ckv_past, kr_past, ckv_new, small_new, wts, zb):
    B, T, _ = qp.shape
    P = ckv_past.shape[1]
    krp = jnp.pad(kr_past, ((0, 0), (0, 0), (KR_LANE, LANES - KR_LANE - D_ROPE)))
    row = lambda r, n: pl.BlockSpec((1, r, n), lambda b: (b, 0, 0))
    full = lambda a: pl.BlockSpec(a.shape, lambda b: (0,) * a.ndim)
    return pl.pallas_call(
        _attn_decode_kernel,
        grid=(B,),
        in_specs=[row(T, H_B * LANES), row(P, KV_LORA), row(P, LANES), row(T, KV_LORA), row(T, LANES),
                  full(wts["wx"]), full(wts["wuv"]), row(T, W_B)],
        out_specs=row(T, W_B),
        out_shape=jax.ShapeDtypeStruct((B, T, W_B), BF16),
        compiler_params=_cparams(1),
        name="attn_decode",
    )(qp, ckv_past, krp, ckv_new, small_new, wts["wx"], wts["wuv"], zb)


def _odd_layer_kernel(layer, nc, x_ref, oa_ref, ob_ref, wa_ref, wb_ref, modp_ref, gpostp_ref,
                      mod_ref, gpre_ref, w_ref, lbl_ref, s0_ref, gn_ref, wout_ref, gpost_ref,
                      xo_ref, sout_ref, qs_ref, lf_ref, iv_ref, z_ref, o_ref, s_scr, *bufs):
    @pl.when(pl.program_id(1) == 0)
    def _():
        s_scr[...] = s0_ref[0]

    mix = _mm(oa_ref[0], wa_ref[...]) + _mm(ob_ref[0], wb_ref[...])
    x = x_ref[0] + modp_ref[0, 2:3, :] * _rms(mix, gpostp_ref[...])
    hb = _prenorm(x, mod_ref, gpre_ref).astype(BF16)
    n = H_C * DK_C
    lg = lbl_ref[...]
    e = jnp.exp(lg - jnp.max(lg, axis=0, keepdims=True))
    p = e / jnp.sum(e, axis=0, keepdims=True)
    lb = jnp.sum(p[1:layer + 1], axis=0, keepdims=True)

    def project(rows, blocks):
        for blk in blocks:
            pr = jnp.dot(hb[rows], w_ref[:, blk * n:(blk + 1) * n], preferred_element_type=F32)
            if blk == 0:
                qs_ref[rows, :] = _silu(pr)
            elif blk == 1:
                lf_ref[rows, :] = jnp.log(lb + (1.0 - lb) * _sigmoid(pr))
            elif blk == 2:
                iv_ref[rows, :] = pr
            else:
                z_ref[rows, :] = pr

    nsub = CHUNK // SUB
    r2, c2 = _iota2((CHUNK, CHUNK))
    cum_l = (c2 <= r2).astype(F32).astype(BF16)
    rm, cm = _iota2((CHUNK, KS_ROWS))
    seg = jnp.zeros_like(cm)
    off = jnp.zeros_like(cm)
    for si in range(1, nsub):
        start = SUB * si * (si + 1) // 2
        seg = seg + (cm >= start).astype(jnp.int32)
        off = off + jnp.where(cm >= start, SUB * si, 0)
    pmask = ((rm >> 4) == seg) & ((cm - off) <= rm) & (cm < KS_USED)

    heads = range(H_C)
    hl = [slice(h * DK_C, (h + 1) * DK_C) for h in heads]

    def prep(ci, buf):
        qt_s, qg_s, kt_s, ks_s, it_s, dec_s = buf
        rows = slice(ci * CHUNK, (ci + 1) * CHUNK)
        lf = lf_ref[rows, :]
        qs = qs_ref[rows, :]
        ivb = iv_ref[rows, :].astype(BF16)
        g = _cum_left(cum_l, lf)
        wloc = jnp.concatenate([g[:SUB]] + [g[j * SUB:(j + 1) * SUB] - g[j * SUB - 1:j * SUB, :]
                                            for j in range(1, CHUNK // SUB)], axis=0)
        k = 1.0 - jnp.exp(lf)
        sub = [slice(j * SUB, (j + 1) * SUB) for j in range(nsub)]
        bound = [g[(j + 1) * SUB - 1:(j + 1) * SUB, :] for j in range(nsub)]
        diag = (k * jnp.exp(jnp.minimum(-wloc, EXP_CLAMP))).astype(BF16)
        base = [k[sub[j]] * jnp.exp(wloc[(j + 1) * SUB - 1:(j + 1) * SUB, :] - wloc[sub[j]]) for j in range(nsub - 1)]
        pieces = []
        for si in range(nsub):
            for j in range(si):
                far = base[j] if j + 1 == si else base[j] * jnp.exp(bound[si - 1] - bound[j])
                pieces.append(far.astype(BF16))
            pieces.append(diag[sub[si]])
        pieces.append(jnp.zeros((KS_ROWS - KS_USED, W_C), BF16))
        ks_s[...] = jnp.concatenate(pieces, axis=0)
        qt_s[...] = (qs * jnp.exp(wloc)).astype(BF16)
        qg_s[...] = (qs * jnp.exp(g)).astype(BF16)
        gl = g[CHUNK - 1:CHUNK, :]
        kt_s[...] = (k * jnp.exp(gl - g)).astype(BF16)
        dec_s[...] = jnp.broadcast_to(jnp.exp(gl), dec_s.shape)
        it_s[...] = jnp.concatenate([ivb[:(si + 1) * SUB] for si in range(nsub)]
                                    + [jnp.zeros((KS_ROWS - KS_USED, W_C), BF16)], axis=0)

    def mat(ci, buf):
        qt_s, qg_s, kt_s, ks_s, it_s, dec_s = buf
        rows = slice(ci * CHUNK, (ci + 1) * CHUNK)
        last = slice(KS_USED - CHUNK, KS_USED)
        st = [s_scr[h] for h in heads]
        sc = [_mm_nt(qt_s[:, hl[h]], ks_s[:, hl[h]]) for h in heads]
        upd = [_mm_tn(it_s[last, hl[h]], kt_s[:, hl[h]]) for h in heads]
        inter = [_mm_nt(qg_s[:, hl[h]], st[h]) for h in heads]
        p = [jnp.where(pmask, x, 0.0).astype(BF16) for x in sc]
        o = [_mm(p[h], it_s[:, hl[h]]) + inter[h] for h in heads]
        for h in heads:
            s_scr[h] = st[h] * dec_s[0:1, hl[h]] + upd[h]
            o_ref[rows, hl[h]] = _rms(o[h], gn_ref[...]) * _silu(z_ref[rows, hl[h]])

    buf_a, buf_b = bufs[:6], bufs[6:]
    tc = nc * CHUNK
    if nc == 1:
        project(slice(0, tc), range(4))
        prep(0, buf_a)
        mat(0, buf_a)
    else:
        npair = nc // 2
        project(slice(0, tc // 2), range(4))
        prep(0, buf_a)
        for i in range(npair):
            if i < npair // 2:
                per = 4 // (npair // 2)
                project(slice(tc // 2, tc), range(i * per, (i + 1) * per))
            c0 = 2 * i
            prep(c0 + 1, buf_b)
            mat(c0, buf_a)
            if c0 + 2 < nc:
                prep(c0 + 2, buf_a)
            mat(c0 + 1, buf_b)
    sout_ref[0] = s_scr[...]
    acc = _mm(o_ref[...], wout_ref[...])
    xo_ref[0] = x + mod_ref[0, 2:3, :] * _rms(acc, gpost_ref[...])


def _odd_layer(x, prev, mod, g_pre, g_post, w_in, w_out, lb_logits, s_hgrn, g_norm_c, layer, nc):
    oa, ob, wa, wb, modp, gpostp = prev
    B, T, _ = x.shape
    assert nc == 1 or nc % 4 == 0
    tc = nc * CHUNK
    operand_bufs = [pltpu.VMEM((CHUNK, W_C), BF16)] * 3 + [pltpu.VMEM((KS_ROWS, W_C), BF16)] * 2 + [
        pltpu.VMEM((8, W_C), F32)]
    tok = pl.BlockSpec((1, tc, D), lambda b, i: (b, i, 0))
    half = pl.BlockSpec((1, tc, W_A), lambda b, i: (b, i, 0))
    modspec = pl.BlockSpec((1, 3, D), lambda b, i: (b, 0, 0))
    st = pl.BlockSpec((1, H_C, DV_C, DK_C), lambda b, i: (b, 0, 0, 0))
    full = lambda a: pl.BlockSpec(a.shape, lambda b, i: (0,) * a.ndim)
    x_new, s_new = pl.pallas_call(
        functools.partial(_odd_layer_kernel, layer, nc),
        grid=(B, T // tc),
        in_specs=[tok, half, half, full(wa), full(wb), modspec, full(gpostp),
                  modspec, full(g_pre), full(w_in), full(lb_logits), st, full(g_norm_c), full(w_out), full(g_post)],
        out_specs=[tok, st],
        out_shape=[jax.ShapeDtypeStruct((B, T, D), F32), jax.ShapeDtypeStruct((B, H_C, DV_C, DK_C), F32)],
        scratch_shapes=[pltpu.VMEM((tc, W_C), F32)] * 5 + [pltpu.VMEM((H_C, DV_C, DK_C), F32)] + operand_bufs * 2,
        compiler_params=_cparams(2),
        name="odd_layer",
    )(x, oa, ob, wa, wb, modp, gpostp, mod, g_pre, w_in, lb_logits, jnp.swapaxes(s_hgrn, -1, -2), g_norm_c, w_out,
      g_post)
    return x_new, jnp.swapaxes(s_new, -1, -2)


def _even_weights(w_in, g_q, w_uq, g_kv, w_ukv):
    o_b, o_a, o_za = QKV_A, QKV_A + H_A, QKV_A + 2 * H_A
    o_cq = o_za + W_A
    o_ckv = o_cq + Q_LORA
    o_kr = o_ckv + KV_LORA
    o_zb = o_kr + D_ROPE
    wbig = jnp.concatenate([w_in[:, :QKV_A], w_in[:, o_za:o_cq], w_in[:, o_cq:o_ckv], w_in[:, o_ckv:o_kr],
                            w_in[:, o_zb:]], axis=1)
    kr = w_in[:, o_kr:o_zb]
    z = lambda n: jnp.zeros((D, n), F32)
    wsm = jnp.concatenate([z(KR_LANE), kr, w_in[:, o_b:o_za], z(LANES - GATE_LANE - 2 * H_A)], axis=1)
    w3 = w_uq.reshape(Q_LORA, H_B, D_NOPE + D_ROPE)
    wq = jnp.concatenate([w3, jnp.zeros((Q_LORA, H_B, LANES - D_NOPE - D_ROPE), F32)], -1)
    wq = wq.reshape(Q_LORA, H_B * LANES)
    k3 = w_ukv.reshape(KV_LORA, H_B, D_NOPE + DV_B)
    wk = jnp.concatenate([k3[..., :D_NOPE], jnp.zeros((KV_LORA, H_B, LANES - D_NOPE), F32)], -1)
    wv = jnp.concatenate([k3[..., D_NOPE:], jnp.zeros((KV_LORA, H_B, LANES - DV_B), F32)], -1)
    wkv = jnp.concatenate([wk.reshape(KV_LORA, H_B * LANES), wv.reshape(KV_LORA, H_B * LANES)], axis=1)
    j = jnp.arange(D_ROPE)
    wx = jnp.zeros((H_B, LANES, KV_LORA + LANES), F32)
    wx = wx.at[:, :D_NOPE, :KV_LORA].set(jnp.transpose(k3[..., :D_NOPE], (1, 2, 0)))
    wx = wx.at[:, D_NOPE + j, KV_LORA + KR_LANE + j].set(1.0)
    hh = jnp.arange(H_B)
    wuv = jnp.zeros((H_B, KV_LORA, H_B, DV_B), F32).at[hh, :, hh, :].set(jnp.transpose(k3[..., D_NOPE:], (1, 0, 2)))
    wuv = wuv.reshape(H_B, KV_LORA, W_B)
    return dict(wbig=wbig.astype(BF16), wsm=wsm.astype(BF16), wq=wq.astype(BF16), wkv=wkv.astype(BF16),
                gq=g_q.reshape(1, Q_LORA), gkv=g_kv.reshape(1, KV_LORA),
                wx=wx.astype(BF16), wuv=wuv.astype(BF16))


def _rope_tables(pos):
    half = D_ROPE // 2
    inv = ROPE_THETA ** (-jnp.arange(half, dtype=F32) / half)
    ang = pos.astype(F32)[:, None] * inv[None, :]
    cos, sin = jnp.cos(ang), jnp.sin(ang)
    t = pos.shape[0]
    one, zero = jnp.ones, jnp.zeros
    scale = (D_NOPE + D_ROPE) ** -0.5 * math.log2(math.e)
    pad_q = LANES - D_NOPE - D_ROPE
    pad_k = LANES - KR_LANE - D_ROPE
    cq = scale * jnp.concatenate([one((t, D_NOPE), F32), cos, cos, zero((t, pad_q), F32)], axis=1)
    sq = scale * jnp.concatenate([zero((t, D_NOPE), F32), -sin, sin, zero((t, pad_q), F32)], axis=1)
    ck = jnp.concatenate([one((t, KR_LANE), F32), cos, cos, one((t, pad_k), F32)], axis=1)
    sk = jnp.concatenate([zero((t, KR_LANE), F32), -sin, sin, zero((t, pad_k), F32)], axis=1)
    return cq, sq, ck, sk


def _tiles(T):
    nc = min(8, T // CHUNK)
    tq = min(512, T)
    return nc, tq


def _trunk(x, mod, ckv_past, kr_past, s_delta, s_conv, s_hgrn, prm, tiles=None):
    B, T, _ = x.shape
    nc, tq = tiles or _tiles(T)
    past = 0 if ckv_past is None else ckv_past.shape[2]
    tabs = _rope_tables(past + jnp.arange(T, dtype=jnp.int32))
    nd, ncv, nk, nr, nh = [], [], [], [], []
    for l in range(DEPTH):
        m = mod[l].reshape(B, 3, D)
        g_pre, g_post = prm["g_pre"][l].reshape(1, D), prm["g_post"][l].reshape(1, D)
        if l % 2 == 0:
            e = l // 2
            wts = prm["even"][e]
            zb, small, ckv, qp, kp, vp, kr_new, o_a, s_new, conv_new = _even_layer(
                x, m, g_pre, wts, tabs, prm["conv_w"][e], s_conv[e], s_delta[e], prm["a_log"][e], prm["dt_bias"][e],
                prm["g_norm_a"][e].reshape(1, DV_A), nc)
            if past:
                assert past % CHUNK == 0 and T <= CHUNK
                o_b = _attn_decode(qp, ckv_past[e], kr_past[e], ckv, small, wts, zb)
            else:
                o_b = _attention(qp, kp, vp, zb, tq)
            w_out = prm["w_out_e"][e]
            pending = (o_a, o_b, w_out[:W_A], w_out[W_A:], m, g_post)
            nd.append(s_new)
            ncv.append(conv_new)
            nk.append(ckv)
            nr.append(kr_new)
        else:
            j = l // 2
            x, s_new = _odd_layer(x, pending, m, g_pre, g_post, prm["w_in_o"][j], prm["w_out_o"][j], prm["lb_logits"],
                                  s_hgrn[j], prm["g_norm_c"][j].reshape(1, DV_C), l, nc)
            nh.append(s_new)
    return (x, jnp.stack(nd), jnp.stack(ncv), jnp.stack(nk), jnp.stack(nr), jnp.stack(nh))


def _forward(x_prompt, x_sample, c_prompt, c_sample, cache_ckv, cache_kr, state_delta, state_conv, state_hgrn,
             w_ada, b_ada, g_pre, g_post, w_in_e, conv_w, a_log, dt_bias, g_norm_a, g_q, w_uq, g_kv, w_ukv,
             w_out_e, w_in_o, lb_logits, g_norm_c, w_out_o, tiles_prompt=None, tiles_sample=None):
    bp, bs = x_prompt.shape[0], x_sample.shape[0]
    n_even, n_odd = w_in_e.shape[0], w_in_o.shape[0]
    mod = _ada(jnp.concatenate([c_prompt, c_sample], axis=0), w_ada, b_ada)
    prm = dict(
        g_pre=g_pre, g_post=g_post, conv_w=conv_w, a_log=a_log, dt_bias=dt_bias, g_norm_a=g_norm_a,
        g_norm_c=g_norm_c, lb_logits=lb_logits,
        even=[_even_weights(w_in_e[e], g_q[e], w_uq[e], g_kv[e], w_ukv[e]) for e in range(n_even)],
        w_out_e=w_out_e.astype(BF16), w_in_o=w_in_o.astype(BF16), w_out_o=w_out_o.astype(BF16))
    zeros = lambda *s: jnp.zeros(s, F32)
    outs_p = _trunk(x_prompt, mod[:, :bp], None, None, zeros(n_even, bp, H_A, DK_A, DV_A),
                    zeros(n_even, bp, CONV_W - 1, QKV_A), zeros(n_odd, bp, H_C, DK_C, DV_C), prm, tiles_prompt)
    outs_s = _trunk(x_sample, mod[:, bp:], cache_ckv, cache_kr, state_delta, state_conv, state_hgrn, prm,
                    tiles_sample)
    return (outs_p[0], outs_s[0]) + outs_p[1:] + outs_s[1:]


def kernel(x_prompt, x_sample, c_prompt, c_sample, cache_ckv, cache_kr, state_delta, state_conv, state_hgrn,
           w_ada, b_ada, g_pre, g_post, w_in_e, conv_w, a_log, dt_bias, g_norm_a, g_q, w_uq, g_kv, w_ukv,
           w_out_e, w_in_o, lb_logits, g_norm_c, w_out_o):
    return _forward(x_prompt, x_sample, c_prompt, c_sample, cache_ckv, cache_kr, state_delta, state_conv,
                    state_hgrn, w_ada, b_ada, g_pre, g_post, w_in_e, conv_w, a_log, dt_bias, g_norm_a, g_q, w_uq,
                    g_kv, w_ukv, w_out_e, w_in_o, lb_logits, g_norm_c, w_out_o)
```

```python
import functools
import math

import jax
import jax.numpy as jnp
from jax import lax
from jax.experimental import pallas as pl
from jax.experimental.pallas import tpu as pltpu

F32 = jnp.float32
BF16 = jnp.bfloat16

D = 1024
DEPTH = 4
CHUNK = 64
EPS = 1e-6
H_A, DK_A, DV_A, CONV_W = 4, 128, 128, 4
H_B, Q_LORA, KV_LORA, D_NOPE, D_ROPE, DV_B = 8, 384, 256, 64, 32, 64
ROPE_THETA = 10000.0
H_C, DK_C, DV_C = 8, 128, 128
W_A, W_B, W_C = H_A * DV_A, H_B * DV_B, H_C * DV_C
QKV_A = H_A * (2 * DK_A + DV_A)
LANES = 128
SUB = 16
KS_USED = SUB * (CHUNK // SUB) * (CHUNK // SUB + 1) // 2
KS_ROWS = 2 * LANES
EXP_CLAMP = 60.0
NEG_BIG = -1e30
VMEM_LIMIT = 56 * 1024 * 1024


def _cparams(n_axes):
    return pltpu.CompilerParams(dimension_semantics=("arbitrary",) * n_axes, vmem_limit_bytes=VMEM_LIMIT)


def _mm(a, b):
    return jnp.dot(a.astype(BF16), b.astype(BF16), preferred_element_type=F32)


def _mm_nt(a, b):
    return lax.dot_general(a.astype(BF16), b.astype(BF16), (((1,), (1,)), ((), ())), preferred_element_type=F32)


def _mm_tn(a, b):
    return lax.dot_general(a.astype(BF16), b.astype(BF16), (((0,), (0,)), ((), ())), preferred_element_type=F32)


def _split3(x):
    hi = x.astype(BF16)
    r1 = x - hi.astype(F32)
    mid = r1.astype(BF16)
    lo = (r1 - mid.astype(F32)).astype(BF16)
    return hi, mid, lo


def _cum_left(l01, x):
    hi, mid, lo = _split3(x)
    d = functools.partial(jnp.dot, preferred_element_type=F32)
    return d(l01, hi) + d(l01, mid) + d(l01, lo)


def _sigmoid(x):
    return 1.0 / (1.0 + jnp.exp(-x))


def _silu(x):
    return x * _sigmoid(x)


def _softplus(x):
    return jnp.maximum(x, 0.0) + jnp.log(1.0 + jnp.exp(-jnp.abs(x)))


def _rms(x, g):
    return x * lax.rsqrt(jnp.mean(x * x, axis=-1, keepdims=True) + EPS) * g


def _prenorm(x, mod_ref, g_ref):
    return _rms(x, g_ref[...]) * (1.0 + mod_ref[0, 1:2, :]) + mod_ref[0, 0:1, :]


def _iota2(shape):
    return lax.broadcasted_iota(jnp.int32, shape, 0), lax.broadcasted_iota(jnp.int32, shape, 1)


def _ada_kernel(c_ref, w_ref, b_ref, o_ref):
    o_ref[0] = _mm(_silu(c_ref[...]), w_ref[0]) + b_ref[0]


def _ada(c, w_ada, b_ada):
    nb = c.shape[0]
    tn = 1024
    return pl.pallas_call(
        _ada_kernel,
        grid=(DEPTH, 3 * D // tn),
        in_specs=[pl.BlockSpec((nb, D), lambda l, j: (0, 0)),
                  pl.BlockSpec((1, D, tn), lambda l, j: (l, 0, j)),
                  pl.BlockSpec((1, 1, tn), lambda l, j: (l, 0, j))],
        out_specs=pl.BlockSpec((1, nb, tn), lambda l, j: (l, 0, j)),
        out_shape=jax.ShapeDtypeStruct((DEPTH, nb, 3 * D), F32),
        compiler_params=_cparams(2),
        name="ada_mod",
    )(c, w_ada, b_ada.reshape(DEPTH, 1, 3 * D))


KR_LANE = 0
GATE_LANE = KR_LANE + D_ROPE


def _value_ones(n):
    lane = lax.broadcasted_iota(jnp.int32, (1, n), 1)
    return ((lane & (LANES - 1)) >= DV_B).astype(F32)


def _unit_lower_inverse(ms, r, c):
    eye = (r == c).astype(F32)
    same16 = (r >> 4) == (c >> 4)
    same32 = (r >> 5) == (c >> 5)
    off16 = same32 & jnp.logical_not(same16)
    md = [jnp.where(same16, m, 0.0) for m in ms]
    m1 = [jnp.where(off16, m, 0.0).astype(BF16) for m in ms]
    m2 = [jnp.where(same32, 0.0, m).astype(BF16) for m in ms]
    p = [eye - x for x in md]
    q = [_mm(x, x) for x in md]
    for step in range(3):
        p = [a + _mm(a, b) for a, b in zip(p, q)]
        if step < 2:
            q = [_mm(b, b) for b in q]
    for mk in (m1, m2):
        t = [_mm(a, b) for a, b in zip(p, mk)]
        p = [a - _mm(b, a) for a, b in zip(p, t)]
    return p


def _even_layer_kernel(nc, x_ref, mod_ref, gpre_ref, wbig_ref, wsm_ref, gq_ref, wq_ref, gkv_ref, wkv_ref,
                       cq_ref, sq_ref, ck_ref, sk_ref, convw_ref, cbuf_ref, s0_ref, rowc_ref, gn_ref,
                       zb_ref, small_ref, ckv_ref, qp_ref, kp_ref, vp_ref, kr_ref, o_ref, sout_ref, cout_ref,
                       xpad, act, za_scr, s_scr):
    tc = nc * CHUNK
    half_rows = [slice(0, tc // 2), slice(tc // 2, tc)] if nc > 1 else [slice(0, tc)]
    half_chunks = [range(0, nc // 2), range(nc // 2, nc)] if nc > 1 else [range(nc)]

    @pl.when(pl.program_id(1) == 0)
    def _():
        xpad[0:8, :] = cbuf_ref[0]
        s_scr[...] = s0_ref[0]

    hb = _prenorm(x_ref[0], mod_ref, gpre_ref).astype(BF16)
    half = D_ROPE // 2
    nq = H_B * LANES

    def rotate_pairs(v, first):
        n = v.shape[1]
        lane = lax.broadcasted_iota(jnp.int32, v.shape, 1) & (LANES - 1)
        return jnp.where(lane < first + half, pltpu.roll(v, n - half, 1), pltpu.roll(v, half, 1))

    def project(rows):
        n0 = rows.start
        big = jnp.dot(hb[rows], wbig_ref[...], preferred_element_type=F32)
        o = 0
        xpad[8 + n0:8 + rows.stop, :] = big[:, o:o + QKV_A]
        o += QKV_A
        za_scr[rows, :] = big[:, o:o + W_A]
        o += W_A
        cq = big[:, o:o + Q_LORA]
        o += Q_LORA
        ckv = big[:, o:o + KV_LORA]
        o += KV_LORA
        zb_ref[0, rows, :] = big[:, o:o + W_B]
        sm = jnp.dot(hb[rows], wsm_ref[...], preferred_element_type=F32)
        small = sm * ck_ref[rows, :] + rotate_pairs(sm, KR_LANE) * sk_ref[rows, :]
        small_ref[0, rows, :] = small
        kr_ref[0, rows, :] = small[:, KR_LANE:KR_LANE + D_ROPE]
        q1 = _mm(_rms(cq, gq_ref[...]), wq_ref[...])
        cq_t = jnp.concatenate([cq_ref[rows, :]] * H_B, axis=1)
        sq_t = jnp.concatenate([sq_ref[rows, :]] * H_B, axis=1)
        qp_ref[0, rows, :] = (q1 * cq_t + rotate_pairs(q1, D_NOPE) * sq_t).astype(BF16)
        cn = _rms(ckv, gkv_ref[...])
        ckv_ref[0, rows, :] = cn
        kv2 = _mm(cn, wkv_ref[...])
        lane = lax.broadcasted_iota(jnp.int32, small.shape, 1)
        kpe = jnp.where((lane >= D_NOPE) & (lane < D_NOPE + D_ROPE), pltpu.roll(small, D_NOPE - KR_LANE, 1), 0.0)
        kp_ref[0, rows, :] = (kv2[:, :nq] + jnp.concatenate([kpe] * H_B, axis=1)).astype(BF16)
        vp_ref[0, rows, :] = (kv2[:, nq:] + _value_ones(nq)).astype(BF16)

    w = convw_ref[...]

    def conv_act(rows):
        a, b = rows.start, rows.stop
        y = (w[0:1] * xpad[5 + a:5 + b, :] + w[1:2] * xpad[6 + a:6 + b, :] + w[2:3] * xpad[7 + a:7 + b, :]
             + w[3:4] * xpad[8 + a:8 + b, :])
        act[rows, :] = _silu(y)

    r, c = _iota2((CHUNK, CHUNK))
    tril = c <= r
    strict = c < r
    tril_b = tril.astype(F32).astype(BF16)
    dtb_row, alog_row = rowc_ref[0:1, :], rowc_ref[1:2, :]

    def l2n(v):
        return v * lax.rsqrt(jnp.sum(v * v, axis=-1, keepdims=True) + EPS)

    def delta(chunks, s):
        items = [(ci, h) for ci in chunks for h in range(H_A)]
        rows = {ci: slice(ci * CHUNK, (ci + 1) * CHUNK) for ci in chunks}
        gc_all, gct, beta = {}, {}, {}
        for ci in chunks:
            sm = small_ref[0, rows[ci], :]
            beta[ci] = _sigmoid(sm)
            gc_all[ci] = _cum_left(tril_b, -jnp.exp(alog_row) * _softplus(sm + dtb_row))
            gct[ci] = jnp.transpose(gc_all[ci])
        qs = [l2n(act[rows[ci], h * DK_A:(h + 1) * DK_A]) * DK_A ** -0.5 for ci, h in items]
        ks = [l2n(act[rows[ci], (H_A + h) * DK_A:(H_A + h + 1) * DK_A]) for ci, h in items]
        vs = [act[rows[ci], 2 * H_A * DK_A + h * DV_A:2 * H_A * DK_A + (h + 1) * DV_A] for ci, h in items]
        bcol = [beta[ci][:, GATE_LANE + h:GATE_LANE + h + 1] for ci, h in items]
        g_lane = GATE_LANE + H_A
        gcol = [gc_all[ci][:, g_lane + h:g_lane + h + 1] for ci, h in items]
        dm = [jnp.exp(jnp.where(tril, gcol[n] - gct[ci][g_lane + h:g_lane + h + 1, :], NEG_BIG))
              for n, (ci, h) in enumerate(items)]
        kb = [k * b for k, b in zip(ks, bcol)]
        kbf = [k.astype(BF16) for k in ks]
        ms = [_mm_nt(a, b) * jnp.where(strict, d, 0.0) for a, b, d in zip(kb, kbf, dm)]
        tinv = _unit_lower_inverse(ms, r, c)
        egc = [jnp.exp(g) for g in gcol]
        uw = [_mm(t, jnp.concatenate([v * b, x * e], axis=1)) for t, v, b, x, e in zip(tinv, vs, bcol, kb, egc)]
        attn = [(_mm_nt(q, k) * d).astype(BF16) for q, k, d in zip(qs, kbf, dm)]
        gl = [g[CHUNK - 1:CHUNK, :] for g in gcol]
        wq = [jnp.concatenate([x[:, DV_A:], q * e], axis=0).astype(BF16) for x, q, e in zip(uw, qs, egc)]
        kgt = [jnp.transpose(k * jnp.exp(l - g)).astype(BF16) for k, l, g in zip(ks, gl, gcol)]
        egl = [jnp.exp(l) for l in gl]
        for pos, ci in enumerate(chunks):
            idx = [pos * H_A + h for h in range(H_A)]
            ws = [_mm(wq[n], s[h]) for h, n in enumerate(idx)]
            v_new = [uw[n][:, :DV_A] - ws[h][:CHUNK] for h, n in enumerate(idx)]
            o = [ws[h][CHUNK:] + _mm(attn[n], v_new[h]) for h, n in enumerate(idx)]
            s = [s[h] * egl[n] + _mm(kgt[n], v_new[h]) for h, n in enumerate(idx)]
            for h in range(H_A):
                lanes = slice(h * DV_A, (h + 1) * DV_A)
                gated = _rms(o[h], gn_ref[...]) * _silu(za_scr[rows[ci], lanes])
                o_ref[0, rows[ci], lanes] = gated.astype(BF16)
        return s

    s = [s_scr[h] for h in range(H_A)]
    project(half_rows[0])
    for hi, rows in enumerate(half_rows):
        if hi + 1 < len(half_rows):
            project(half_rows[hi + 1])
        conv_act(rows)
        s = delta(half_chunks[hi], s)
    tail = xpad[tc:tc + 8, :]
    cout_ref[0] = tail
    xpad[0:8, :] = tail
    for h in range(H_A):
        s_scr[h] = s[h]
    sout_ref[0] = s_scr[...]


def _even_layer(x, mod, g_pre, wts, tabs, conv_w, conv_buf, s_delta, a_log, dt_bias, g_norm_a, nc):
    B, T, _ = x.shape
    assert nc == 1 or nc % 2 == 0
    tc = nc * CHUNK
    lane = jnp.zeros((LANES,), F32)
    a_lanes = slice(GATE_LANE + H_A, GATE_LANE + 2 * H_A)
    rowc = jnp.stack([lane.at[a_lanes].set(dt_bias), lane.at[a_lanes].set(a_log)])
    cbuf = jnp.pad(conv_buf, ((0, 0), (8 - (CONV_W - 1), 0), (0, 0)))
    full = lambda a: pl.BlockSpec(a.shape, lambda b, i: (0,) * a.ndim)
    tok = lambda n: pl.BlockSpec((1, tc, n), lambda b, i: (b, i, 0))
    tab = pl.BlockSpec((tc, LANES), lambda b, i: (i, 0))
    state = pl.BlockSpec((1, H_A, DK_A, DV_A), lambda b, i: (b, 0, 0, 0))
    tail = pl.BlockSpec((1, 8, QKV_A), lambda b, i: (b, 0, 0))
    outs = [(W_B, F32), (LANES, F32), (KV_LORA, F32), (H_B * LANES, BF16), (H_B * LANES, BF16), (H_B * LANES, BF16),
            (D_ROPE, F32), (W_A, BF16)]
    res = pl.pallas_call(
        functools.partial(_even_layer_kernel, nc),
        grid=(B, T // tc),
        in_specs=[tok(D), pl.BlockSpec((1, 3, D), lambda b, i: (b, 0, 0)), full(g_pre),
                  full(wts["wbig"]), full(wts["wsm"]), full(wts["gq"]), full(wts["wq"]), full(wts["gkv"]),
                  full(wts["wkv"]), tab, tab, tab, tab, full(conv_w), tail, state, full(rowc), full(g_norm_a)],
        out_specs=[tok(n) for n, _ in outs] + [state, tail],
        out_shape=[jax.ShapeDtypeStruct((B, T, n), dt) for n, dt in outs]
        + [jax.ShapeDtypeStruct((B, H_A, DK_A, DV_A), F32), jax.ShapeDtypeStruct((B, 8, QKV_A), F32)],
        scratch_shapes=[pltpu.VMEM((tc + 8, QKV_A), F32), pltpu.VMEM((tc, QKV_A), F32), pltpu.VMEM((tc, W_A), F32),
                        pltpu.VMEM((H_A, DK_A, DV_A), F32)],
        compiler_params=_cparams(2),
        name="even_layer",
    )(x, mod, g_pre, wts["wbig"], wts["wsm"], wts["gq"], wts["wq"], wts["gkv"], wts["wkv"], *tabs, conv_w, cbuf,
      s_delta, rowc, g_norm_a)
    zb, small, ckv, qp, kp, vp, kr_new, o_a, s_new, ctail = res
    return zb, small, ckv, qp, kp, vp, kr_new, o_a, s_new, ctail[:, 8 - (CONV_W - 1):, :]


def _attn_kernel(tq, q_ref, k_ref, v_ref, zb_ref, o_ref, m_scr, acc_scr):
    i = pl.program_id(2)
    heads = (0, 1)
    hl = [slice(hh * LANES, (hh + 1) * LANES) for hh in heads]
    r, c = _iota2((tq, tq))
    mask = (c >> 6) <= (r >> 6)
    qs = [q_ref[0, :, hl[hh]] for hh in heads]
    m_scr[...] = jnp.full((2, tq, LANES), NEG_BIG, F32)
    acc_scr[...] = jnp.zeros((2, tq, LANES), F32)

    def tile(rows, tk, masked):
        s = [_mm_nt(qs[hh], k_ref[0, rows, hl[hh]]) for hh in heads]
        if masked:
            s = [jnp.where(mask, x, NEG_BIG) for x in s]
        m_prev = [m_scr[hh] for hh in heads]
        m_new = [jnp.maximum(m_prev[hh], jnp.max(s[hh], axis=-1, keepdims=True)) for hh in heads]
        alpha = [jnp.exp2(m_prev[hh] - m_new[hh]) for hh in heads]
        p = [jnp.exp2(s[hh] - jnp.concatenate([m_new[hh]] * (tk // LANES), axis=1)) for hh in heads]
        pv = [_mm(p[hh], v_ref[0, rows, hl[hh]]) for hh in heads]
        for hh in heads:
            acc_scr[hh] = alpha[hh] * acc_scr[hh] + pv[hh]
            m_scr[hh] = m_new[hh]

    def body(j, carry):
        tile(pl.ds(pl.multiple_of(j * 2 * tq, tq), 2 * tq), 2 * tq, False)
        return carry
    lax.fori_loop(0, i // 2, body, 0)

    @pl.when(i % 2 == 1)
    def _():
        tile(pl.ds(pl.multiple_of((i - 1) * tq, tq), tq), tq, False)

    tile(pl.ds(pl.multiple_of(i * tq, tq), tq), tq, True)
    _, lane = _iota2((tq, LANES))
    a0, a1 = acc_scr[0], acc_scr[1]
    o0 = a0 / pltpu.roll(a0, DV_B, 1)
    o1 = a1 / pltpu.roll(a1, DV_B, 1)
    o = jnp.where(lane < DV_B, o0, pltpu.roll(o1, DV_B, 1))
    o_ref[0] = (o * _silu(zb_ref[0])).astype(BF16)


def _attention(qp, kp, vp, zb, tq):
    B, T, _ = qp.shape
    pair = lambda rows, im: pl.BlockSpec((1, rows, 2 * LANES), im)
    return pl.pallas_call(
        functools.partial(_attn_kernel, tq),
        grid=(B, H_B // 2, T // tq),
        in_specs=[pair(tq, lambda b, h, i: (b, i, h)), pair(T, lambda b, h, i: (b, 0, h)),
                  pair(T, lambda b, h, i: (b, 0, h)), pl.BlockSpec((1, tq, LANES), lambda b, h, i: (b, i, h))],
        out_specs=pl.BlockSpec((1, tq, LANES), lambda b, h, i: (b, i, h)),
        out_shape=jax.ShapeDtypeStruct((B, T, W_B), BF16),
        scratch_shapes=[pltpu.VMEM((2, tq, LANES), F32)] * 2,
        compiler_params=_cparams(3),
        name="attn",
    )(qp, kp, vp, zb)


def _attn_decode_kernel(q_ref, cpast_ref, krpast_ref, cnew_ref, small_ref, wx_ref, wuv_ref, zb_ref, o_ref):
    t = q_ref.shape[1]
    q = q_ref[0]
    qx = jnp.concatenate([_mm(q[:, h * LANES:(h + 1) * LANES], wx_ref[h]) for h in range(H_B)], axis=0)
    ql, qr = qx[:, :KV_LORA].astype(BF16), qx[:, KV_LORA:].astype(BF16)
    segs = [(cpast_ref[0, 0].astype(BF16), krpast_ref[0, 0].astype(BF16)),
            (cnew_ref[0].astype(BF16), small_ref[0].astype(BF16))]
    s = [_mm_nt(ql, c) + _mm_nt(qr, kr) for c, kr in segs]
    m = jnp.maximum(jnp.max(s[0], axis=-1, keepdims=True), jnp.max(s[1], axis=-1, keepdims=True))
    p = [jnp.exp2(x - m) for x in s]
    l = jnp.sum(p[0], axis=-1, keepdims=True) + jnp.sum(p[1], axis=-1, keepdims=True)
    ol = (_mm(p[0], segs[0][0]) + _mm(p[1], segs[1][0])) / l
    o = _mm(ol[0:t], wuv_ref[0])
    for h in range(1, H_B):
        o = o + _mm(ol[h * t:(h + 1) * t], wuv_ref[h])
    o_ref[0] = (o * _silu(zb_ref[0])).astype(BF16)


def _attn_decode(qp, ckv_cache, krp_cache, e, ckv_new, small_new, wts, zb):
    B, T, _ = qp.shape
    P = ckv_cache.shape[2]
    row = lambda r, n: pl.BlockSpec((1, r, n), lambda b: (b, 0, 0))
    cache = lambda n: pl.BlockSpec((1, 1, P, n), lambda b: (e, b, 0, 0))
    full = lambda a: pl.BlockSpec(a.shape, lambda b: (0,) * a.ndim)
    return pl.pallas_call(
        _attn_decode_kernel,
        grid=(B,),
        in_specs=[row(T, H_B * LANES), cache(KV_LORA), cache(LANES), row(T, KV_LORA), row(T, LANES),
                  full(wts["wx"]), full(wts["wuv"]), row(T, W_B)],
        out_specs=row(T, W_B),
        out_shape=jax.ShapeDtypeStruct((B, T, W_B), BF16),
        compiler_params=_cparams(1),
        name="attn_decode",
    )(qp, ckv_cache, krp_cache, ckv_new, small_new, wts["wx"], wts["wuv"], zb)


def _odd_layer_kernel(layer, nc, x_ref, oa_ref, ob_ref, wa_ref, wb_ref, modp_ref, gpostp_ref,
                      mod_ref, gpre_ref, w_ref, lbl_ref, s0_ref, gn_ref, wout_ref, gpost_ref,
                      xo_ref, sout_ref, qs_ref, lf_ref, iv_ref, z_ref, o_ref, s_scr, *bufs):
    @pl.when(pl.program_id(1) == 0)
    def _():
        s_scr[...] = s0_ref[0]

    mix = _mm(oa_ref[0], wa_ref[...]) + _mm(ob_ref[0], wb_ref[...])
    x = x_ref[0] + modp_ref[0, 2:3, :] * _rms(mix, gpostp_ref[...])
    hb = _prenorm(x, mod_ref, gpre_ref).astype(BF16)
    n = H_C * DK_C
    lg = lbl_ref[...]
    e = jnp.exp(lg - jnp.max(lg, axis=0, keepdims=True))
    p = e / jnp.sum(e, axis=0, keepdims=True)
    lb = jnp.sum(p[1:layer + 1], axis=0, keepdims=True)

    def project(rows, blocks):
        for blk in blocks:
            pr = jnp.dot(hb[rows], w_ref[:, blk * n:(blk + 1) * n], preferred_element_type=F32)
            if blk == 0:
                qs_ref[rows, :] = _silu(pr)
            elif blk == 1:
                lf_ref[rows, :] = jnp.log(lb + (1.0 - lb) * _sigmoid(pr))
            elif blk == 2:
                iv_ref[rows, :] = pr
            else:
                z_ref[rows, :] = pr

    nsub = CHUNK // SUB
    r2, c2 = _iota2((CHUNK, CHUNK))
    cum_l = (c2 <= r2).astype(F32).astype(BF16)
    rm, cm = _iota2((CHUNK, KS_ROWS))
    seg = jnp.zeros_like(cm)
    off = jnp.zeros_like(cm)
    for si in range(1, nsub):
        start = SUB * si * (si + 1) // 2
        seg = seg + (cm >= start).astype(jnp.int32)
        off = off + jnp.where(cm >= start, SUB * si, 0)
    pmask = ((rm >> 4) == seg) & ((cm - off) <= rm) & (cm < KS_USED)

    heads = range(H_C)
    hl = [slice(h * DK_C, (h + 1) * DK_C) for h in heads]

    def prep(ci, buf):
        qt_s, qg_s, kt_s, ks_s, it_s, dec_s = buf
        rows = slice(ci * CHUNK, (ci + 1) * CHUNK)
        lf = lf_ref[rows, :]
        qs = qs_ref[rows, :]
        ivb = iv_ref[rows, :].astype(BF16)
        g = _cum_left(cum_l, lf)
        wloc = jnp.concatenate([g[:SUB]] + [g[j * SUB:(j + 1) * SUB] - g[j * SUB - 1:j * SUB, :]
                                            for j in range(1, CHUNK // SUB)], axis=0)
        k = 1.0 - jnp.exp(lf)
        sub = [slice(j * SUB, (j + 1) * SUB) for j in range(nsub)]
        bound = [g[(j + 1) * SUB - 1:(j + 1) * SUB, :] for j in range(nsub)]
        diag = (k * jnp.exp(jnp.minimum(-wloc, EXP_CLAMP))).astype(BF16)
        base = [k[sub[j]] * jnp.exp(wloc[(j + 1) * SUB - 1:(j + 1) * SUB, :] - wloc[sub[j]]) for j in range(nsub - 1)]
        pieces = []
        for si in range(nsub):
            for j in range(si):
                far = base[j] if j + 1 == si else base[j] * jnp.exp(bound[si - 1] - bound[j])
                pieces.append(far.astype(BF16))
            pieces.append(diag[sub[si]])
        pieces.append(jnp.zeros((KS_ROWS - KS_USED, W_C), BF16))
        ks_s[...] = jnp.concatenate(pieces, axis=0)
        qt_s[...] = (qs * jnp.exp(wloc)).astype(BF16)
        qg_s[...] = (qs * jnp.exp(g)).astype(BF16)
        gl = g[CHUNK - 1:CHUNK, :]
        kt_s[...] = (k * jnp.exp(gl - g)).astype(BF16)
        dec_s[...] = jnp.broadcast_to(jnp.exp(gl), dec_s.shape)
        it_s[...] = jnp.concatenate([ivb[:(si + 1) * SUB] for si in range(nsub)]
                                    + [jnp.zeros((KS_ROWS - KS_USED, W_C), BF16)], axis=0)

    def mat(ci, buf):
        qt_s, qg_s, kt_s, ks_s, it_s, dec_s = buf
        rows = slice(ci * CHUNK, (ci + 1) * CHUNK)
        last = slice(KS_USED - CHUNK, KS_USED)
        st = [s_scr[h] for h in heads]
        sc = [_mm_nt(qt_s[:, hl[h]], ks_s[:, hl[h]]) for h in heads]
        upd = [_mm_tn(it_s[last, hl[h]], kt_s[:, hl[h]]) for h in heads]
        inter = [_mm_nt(qg_s[:, hl[h]], st[h]) for h in heads]
        p = [jnp.where(pmask, x, 0.0).astype(BF16) for x in sc]
        o = [_mm(p[h], it_s[:, hl[h]]) + inter[h] for h in heads]
        for h in heads:
            s_scr[h] = st[h] * dec_s[0:1, hl[h]] + upd[h]
            o_ref[rows, hl[h]] = _rms(o[h], gn_ref[...]) * _silu(z_ref[rows, hl[h]])

    buf_a, buf_b = bufs[:6], bufs[6:]
    tc = nc * CHUNK
    if nc == 1:
        project(slice(0, tc), range(4))
        prep(0, buf_a)
        mat(0, buf_a)
    else:
        npair = nc // 2
        project(slice(0, tc // 2), range(4))
        prep(0, buf_a)
        for i in range(npair):
            if i < npair // 2:
                per = 4 // (npair // 2)
                project(slice(tc // 2, tc), range(i * per, (i + 1) * per))
            c0 = 2 * i
            prep(c0 + 1, buf_b)
            mat(c0, buf_a)
            if c0 + 2 < nc:
                prep(c0 + 2, buf_a)
            mat(c0 + 1, buf_b)
    sout_ref[0] = s_scr[...]
    acc = _mm(o_ref[...], wout_ref[...])
    xo_ref[0] = x + mod_ref[0, 2:3, :] * _rms(acc, gpost_ref[...])


def _odd_layer(x, prev, mod, g_pre, g_post, w_in, w_out, lb_logits, s_hgrn, g_norm_c, layer, nc):
    oa, ob, wa, wb, modp, gpostp = prev
    B, T, _ = x.shape
    assert nc == 1 or nc % 4 == 0
    tc = nc * CHUNK
    operand_bufs = [pltpu.VMEM((CHUNK, W_C), BF16)] * 3 + [pltpu.VMEM((KS_ROWS, W_C), BF16)] * 2 + [
        pltpu.VMEM((8, W_C), F32)]
    tok = pl.BlockSpec((1, tc, D), lambda b, i: (b, i, 0))
    half = pl.BlockSpec((1, tc, W_A), lambda b, i: (b, i, 0))
    modspec = pl.BlockSpec((1, 3, D), lambda b, i: (b, 0, 0))
    st = pl.BlockSpec((1, H_C, DV_C, DK_C), lambda b, i: (b, 0, 0, 0))
    full = lambda a: pl.BlockSpec(a.shape, lambda b, i: (0,) * a.ndim)
    x_new, s_new = pl.pallas_call(
        functools.partial(_odd_layer_kernel, layer, nc),
        grid=(B, T // tc),
        in_specs=[tok, half, half, full(wa), full(wb), modspec, full(gpostp),
                  modspec, full(g_pre), full(w_in), full(lb_logits), st, full(g_norm_c), full(w_out), full(g_post)],
        out_specs=[tok, st],
        out_shape=[jax.ShapeDtypeStruct((B, T, D), F32), jax.ShapeDtypeStruct((B, H_C, DV_C, DK_C), F32)],
        scratch_shapes=[pltpu.VMEM((tc, W_C), F32)] * 5 + [pltpu.VMEM((H_C, DV_C, DK_C), F32)] + operand_bufs * 2,
        compiler_params=_cparams(2),
        name="odd_layer",
    )(x, oa, ob, wa, wb, modp, gpostp, mod, g_pre, w_in, lb_logits, jnp.swapaxes(s_hgrn, -1, -2), g_norm_c, w_out,
      g_post)
    return x_new, jnp.swapaxes(s_new, -1, -2)


def _even_weights(w_in, g_q, w_uq, g_kv, w_ukv):
    o_b, o_a, o_za = QKV_A, QKV_A + H_A, QKV_A + 2 * H_A
    o_cq = o_za + W_A
    o_ckv = o_cq + Q_LORA
    o_kr = o_ckv + KV_LORA
    o_zb = o_kr + D_ROPE
    wbig = jnp.concatenate([w_in[:, :QKV_A], w_in[:, o_za:o_cq], w_in[:, o_cq:o_ckv], w_in[:, o_ckv:o_kr],
                            w_in[:, o_zb:]], axis=1)
    kr = w_in[:, o_kr:o_zb]
    z = lambda n: jnp.zeros((D, n), F32)
    wsm = jnp.concatenate([z(KR_LANE), kr, w_in[:, o_b:o_za], z(LANES - GATE_LANE - 2 * H_A)], axis=1)
    w3 = w_uq.reshape(Q_LORA, H_B, D_NOPE + D_ROPE)
    wq = jnp.concatenate([w3, jnp.zeros((Q_LORA, H_B, LANES - D_NOPE - D_ROPE), F32)], -1)
    wq = wq.reshape(Q_LORA, H_B * LANES)
    k3 = w_ukv.reshape(KV_LORA, H_B, D_NOPE + DV_B)
    wk = jnp.concatenate([k3[..., :D_NOPE], jnp.zeros((KV_LORA, H_B, LANES - D_NOPE), F32)], -1)
    wv = jnp.concatenate([k3[..., D_NOPE:], jnp.zeros((KV_LORA, H_B, LANES - DV_B), F32)], -1)
    wkv = jnp.concatenate([wk.reshape(KV_LORA, H_B * LANES), wv.reshape(KV_LORA, H_B * LANES)], axis=1)
    nx = KV_LORA + LANES
    place = jnp.eye(D_ROPE, nx, k=KV_LORA + KR_LANE, dtype=F32)
    wx = jnp.concatenate([
        jnp.pad(jnp.transpose(k3[..., :D_NOPE], (1, 2, 0)), ((0, 0), (0, 0), (0, nx - KV_LORA))),
        jnp.broadcast_to(place, (H_B, D_ROPE, nx)),
        jnp.zeros((H_B, LANES - D_NOPE - D_ROPE, nx), F32)], axis=1)
    uv = jnp.transpose(k3[..., D_NOPE:], (1, 0, 2))
    wuv = (uv[:, :, None, :] * jnp.eye(H_B, dtype=F32)[:, None, :, None]).reshape(H_B, KV_LORA, W_B)
    return dict(wbig=wbig.astype(BF16), wsm=wsm.astype(BF16), wq=wq.astype(BF16), wkv=wkv.astype(BF16),
                gq=g_q.reshape(1, Q_LORA), gkv=g_kv.reshape(1, KV_LORA),
                wx=wx.astype(BF16), wuv=wuv.astype(BF16))


def _rope_tables(pos):
    half = D_ROPE // 2
    inv = ROPE_THETA ** (-jnp.arange(half, dtype=F32) / half)
    ang = pos.astype(F32)[:, None] * inv[None, :]
    cos, sin = jnp.cos(ang), jnp.sin(ang)
    t = pos.shape[0]
    one, zero = jnp.ones, jnp.zeros
    scale = (D_NOPE + D_ROPE) ** -0.5 * math.log2(math.e)
    pad_q = LANES - D_NOPE - D_ROPE
    pad_k = LANES - KR_LANE - D_ROPE
    cq = scale * jnp.concatenate([one((t, D_NOPE), F32), cos, cos, zero((t, pad_q), F32)], axis=1)
    sq = scale * jnp.concatenate([zero((t, D_NOPE), F32), -sin, sin, zero((t, pad_q), F32)], axis=1)
    ck = jnp.concatenate([one((t, KR_LANE), F32), cos, cos, one((t, pad_k), F32)], axis=1)
    sk = jnp.concatenate([zero((t, KR_LANE), F32), -sin, sin, zero((t, pad_k), F32)], axis=1)
    return cq, sq, ck, sk


def _tiles(T):
    nc = min(8, T // CHUNK)
    tq = min(512, T)
    return nc, tq


def _trunk(x, mod, ckv_past, kr_past, s_delta, s_conv, s_hgrn, prm, tiles=None):
    B, T, _ = x.shape
    nc, tq = tiles or _tiles(T)
    past = 0 if ckv_past is None else ckv_past.shape[2]
    tabs = _rope_tables(past + jnp.arange(T, dtype=jnp.int32))
    if past:
        krp_past = jnp.pad(kr_past, ((0, 0), (0, 0), (0, 0), (KR_LANE, LANES - KR_LANE - D_ROPE)))
    nd, ncv, nk, nr, nh = [], [], [], [], []
    for l in range(DEPTH):
        m = mod[l].reshape(B, 3, D)
        g_pre, g_post = prm["g_pre"][l].reshape(1, D), prm["g_post"][l].reshape(1, D)
        if l % 2 == 0:
            e = l // 2
            wts = prm["even"][e]
            zb, small, ckv, qp, kp, vp, kr_new, o_a, s_new, conv_new = _even_layer(
                x, m, g_pre, wts, tabs, prm["conv_w"][e], s_conv[e], s_delta[e], prm["a_log"][e], prm["dt_bias"][e],
                prm["g_norm_a"][e].reshape(1, DV_A), nc)
            if past:
                assert past % CHUNK == 0 and T <= CHUNK
                o_b = _attn_decode(qp, ckv_past, krp_past, e, ckv, small, wts, zb)
            else:
                o_b = _attention(qp, kp, vp, zb, tq)
            w_out = prm["w_out_e"][e]
            pending = (o_a, o_b, w_out[:W_A], w_out[W_A:], m, g_post)
            nd.append(s_new)
            ncv.append(conv_new)
            nk.append(ckv)
            nr.append(kr_new)
        else:
            j = l // 2
            x, s_new = _odd_layer(x, pending, m, g_pre, g_post, prm["w_in_o"][j], prm["w_out_o"][j], prm["lb_logits"],
                                  s_hgrn[j], prm["g_norm_c"][j].reshape(1, DV_C), l, nc)
            nh.append(s_new)
    return (x, jnp.stack(nd), jnp.stack(ncv), jnp.stack(nk), jnp.stack(nr), jnp.stack(nh))


def _forward(x_prompt, x_sample, c_prompt, c_sample, cache_ckv, cache_kr, state_delta, state_conv, state_hgrn,
             w_ada, b_ada, g_pre, g_post, w_in_e, conv_w, a_log, dt_bias, g_norm_a, g_q, w_uq, g_kv, w_ukv,
             w_out_e, w_in_o, lb_logits, g_norm_c, w_out_o, tiles_prompt=None, tiles_sample=None):
    bp, bs = x_prompt.shape[0], x_sample.shape[0]
    n_even, n_odd = w_in_e.shape[0], w_in_o.shape[0]
    mod = _ada(jnp.concatenate([c_prompt, c_sample], axis=0), w_ada, b_ada)
    prm = dict(
        g_pre=g_pre, g_post=g_post, conv_w=conv_w, a_log=a_log, dt_bias=dt_bias, g_norm_a=g_norm_a,
        g_norm_c=g_norm_c, lb_logits=lb_logits,
        even=[_even_weights(w_in_e[e], g_q[e], w_uq[e], g_kv[e], w_ukv[e]) for e in range(n_even)],
        w_out_e=w_out_e.astype(BF16), w_in_o=w_in_o.astype(BF16), w_out_o=w_out_o.astype(BF16))
    zeros = lambda *s: jnp.zeros(s, F32)
    outs_p = _trunk(x_prompt, mod[:, :bp], None, None, zeros(n_even, bp, H_A, DK_A, DV_A),
                    zeros(n_even, bp, CONV_W - 1, QKV_A), zeros(n_odd, bp, H_C, DK_C, DV_C), prm, tiles_prompt)
    outs_s = _trunk(x_sample, mod[:, bp:], cache_ckv, cache_kr, state_delta, state_conv, state_hgrn, prm,
                    tiles_sample)
    return (outs_p[0], outs_s[0]) + outs_p[1:] + outs_s[1:]


def kernel(x_prompt, x_sample, c_prompt, c_sample, cache_ckv, cache_kr, state_delta, state_conv, state_hgrn,
           w_ada, b_ada, g_pre, g_post, w_in_e, conv_w, a_log, dt_bias, g_norm_a, g_q, w_uq, g_kv, w_ukv,
           w_out_e, w_in_o, lb_logits, g_norm_c, w_out_o):
    return _forward(x_prompt, x_sample, c_prompt, c_sample, cache_ckv, cache_kr, state_delta, state_conv,
                    state_hgrn, w_ada, b_ada, g_pre, g_post, w_in_e, conv_w, a_log, dt_bias, g_norm_a, g_q, w_uq,
                    g_kv, w_ukv, w_out_e, w_in_o, lb_logits, g_norm_c, w_out_o)
```

```python
import functools
import math

import jax
import jax.numpy as jnp
from jax import lax
from jax.experimental import pallas as pl
from jax.experimental.pallas import tpu as pltpu

F32 = jnp.float32
BF16 = jnp.bfloat16

D = 1024
DEPTH = 4
CHUNK = 64
EPS = 1e-6
H_A, DK_A, DV_A, CONV_W = 4, 128, 128, 4
H_B, Q_LORA, KV_LORA, D_NOPE, D_ROPE, DV_B = 8, 384, 256, 64, 32, 64
ROPE_THETA = 10000.0
H_C, DK_C, DV_C = 8, 128, 128
W_A, W_B, W_C = H_A * DV_A, H_B * DV_B, H_C * DV_C
QKV_A = H_A * (2 * DK_A + DV_A)
LANES = 128
SUB = 16
KS_USED = SUB * (CHUNK // SUB) * (CHUNK // SUB + 1) // 2
KS_ROWS = 2 * LANES
EXP_CLAMP = 60.0
NEG_BIG = -1e30
VMEM_LIMIT = 56 * 1024 * 1024


def _cparams(n_axes):
    return pltpu.CompilerParams(dimension_semantics=("arbitrary",) * n_axes, vmem_limit_bytes=VMEM_LIMIT)


def _mm(a, b):
    return jnp.dot(a.astype(BF16), b.astype(BF16), preferred_element_type=F32)


def _mm_nt(a, b):
    return lax.dot_general(a.astype(BF16), b.astype(BF16), (((1,), (1,)), ((), ())), preferred_element_type=F32)


def _mm_tn(a, b):
    return lax.dot_general(a.astype(BF16), b.astype(BF16), (((0,), (0,)), ((), ())), preferred_element_type=F32)


def _split3(x):
    hi = x.astype(BF16)
    r1 = x - hi.astype(F32)
    mid = r1.astype(BF16)
    lo = (r1 - mid.astype(F32)).astype(BF16)
    return hi, mid, lo


def _cum_left(l01, x):
    hi, mid, lo = _split3(x)
    d = functools.partial(jnp.dot, preferred_element_type=F32)
    return d(l01, hi) + d(l01, mid) + d(l01, lo)


def _sigmoid(x):
    return 1.0 / (1.0 + jnp.exp(-x))


def _silu(x):
    return x * _sigmoid(x)


def _softplus(x):
    return jnp.maximum(x, 0.0) + jnp.log(1.0 + jnp.exp(-jnp.abs(x)))


def _rms(x, g):
    return x * lax.rsqrt(jnp.mean(x * x, axis=-1, keepdims=True) + EPS) * g


def _prenorm(x, mod_ref, g_ref):
    return _rms(x, g_ref[...]) * (1.0 + mod_ref[0, 1:2, :]) + mod_ref[0, 0:1, :]


def _iota2(shape):
    return lax.broadcasted_iota(jnp.int32, shape, 0), lax.broadcasted_iota(jnp.int32, shape, 1)


def _ada_kernel(c_ref, w_ref, b_ref, o_ref):
    o_ref[0] = _mm(_silu(c_ref[...]), w_ref[0]) + b_ref[0]


def _ada(c, w_ada, b_ada):
    nb = c.shape[0]
    tn = 1024
    return pl.pallas_call(
        _ada_kernel,
        grid=(DEPTH, 3 * D // tn),
        in_specs=[pl.BlockSpec((nb, D), lambda l, j: (0, 0)),
                  pl.BlockSpec((1, D, tn), lambda l, j: (l, 0, j)),
                  pl.BlockSpec((1, 1, tn), lambda l, j: (l, 0, j))],
        out_specs=pl.BlockSpec((1, nb, tn), lambda l, j: (l, 0, j)),
        out_shape=jax.ShapeDtypeStruct((DEPTH, nb, 3 * D), F32),
        compiler_params=_cparams(2),
        name="ada_mod",
    )(c, w_ada, b_ada.reshape(DEPTH, 1, 3 * D))


KR_LANE = 0
GATE_LANE = KR_LANE + D_ROPE


def _value_ones(n):
    lane = lax.broadcasted_iota(jnp.int32, (1, n), 1)
    return ((lane & (LANES - 1)) >= DV_B).astype(F32)


def _unit_lower_inverse(ms, r, c):
    eye = (r == c).astype(F32)
    same16 = (r >> 4) == (c >> 4)
    same32 = (r >> 5) == (c >> 5)
    off16 = same32 & jnp.logical_not(same16)
    md = [jnp.where(same16, m, 0.0) for m in ms]
    m1 = [jnp.where(off16, m, 0.0).astype(BF16) for m in ms]
    m2 = [jnp.where(same32, 0.0, m).astype(BF16) for m in ms]
    p = [eye - x for x in md]
    q = [_mm(x, x) for x in md]
    for step in range(3):
        p = [a + _mm(a, b) for a, b in zip(p, q)]
        if step < 2:
            q = [_mm(b, b) for b in q]
    for mk in (m1, m2):
        t = [_mm(a, b) for a, b in zip(p, mk)]
        p = [a - _mm(b, a) for a, b in zip(p, t)]
    return p


def _even_layer_kernel(nc, x_ref, mod_ref, gpre_ref, wbig_ref, wsm_ref, gq_ref, wq_ref, gkv_ref, wkv_ref,
                       cq_ref, sq_ref, ck_ref, sk_ref, convw_ref, cbuf_ref, s0_ref, rowc_ref, gn_ref,
                       zb_ref, small_ref, ckv_ref, qp_ref, kp_ref, vp_ref, kr_ref, o_ref, sout_ref, cout_ref,
                       xpad, act, za_scr, s_scr):
    tc = nc * CHUNK
    half_rows = [slice(0, tc // 2), slice(tc // 2, tc)] if nc > 1 else [slice(0, tc)]
    half_chunks = [range(0, nc // 2), range(nc // 2, nc)] if nc > 1 else [range(nc)]

    @pl.when(pl.program_id(1) == 0)
    def _():
        xpad[0:8, :] = cbuf_ref[0]
        s_scr[...] = s0_ref[0]

    hb = _prenorm(x_ref[0], mod_ref, gpre_ref).astype(BF16)
    half = D_ROPE // 2
    nq = H_B * LANES

    def rotate_pairs(v, first):
        n = v.shape[1]
        lane = lax.broadcasted_iota(jnp.int32, v.shape, 1) & (LANES - 1)
        return jnp.where(lane < first + half, pltpu.roll(v, n - half, 1), pltpu.roll(v, half, 1))

    def project(rows):
        n0 = rows.start
        big = jnp.dot(hb[rows], wbig_ref[...], preferred_element_type=F32)
        o = 0
        xpad[8 + n0:8 + rows.stop, :] = big[:, o:o + QKV_A]
        o += QKV_A
        za_scr[rows, :] = big[:, o:o + W_A]
        o += W_A
        cq = big[:, o:o + Q_LORA]
        o += Q_LORA
        ckv = big[:, o:o + KV_LORA]
        o += KV_LORA
        zb_ref[0, rows, :] = big[:, o:o + W_B]
        sm = jnp.dot(hb[rows], wsm_ref[...], preferred_element_type=F32)
        small = sm * ck_ref[rows, :] + rotate_pairs(sm, KR_LANE) * sk_ref[rows, :]
        small_ref[0, rows, :] = small
        kr_ref[0, rows, :] = small[:, KR_LANE:KR_LANE + D_ROPE]
        q1 = _mm(_rms(cq, gq_ref[...]), wq_ref[...])
        cq_t = jnp.concatenate([cq_ref[rows, :]] * H_B, axis=1)
        sq_t = jnp.concatenate([sq_ref[rows, :]] * H_B, axis=1)
        qp_ref[0, rows, :] = (q1 * cq_t + rotate_pairs(q1, D_NOPE) * sq_t).astype(BF16)
        cn = _rms(ckv, gkv_ref[...])
        ckv_ref[0, rows, :] = cn
        kv2 = _mm(cn, wkv_ref[...])
        lane = lax.broadcasted_iota(jnp.int32, small.shape, 1)
        kpe = jnp.where((lane >= D_NOPE) & (lane < D_NOPE + D_ROPE), pltpu.roll(small, D_NOPE - KR_LANE, 1), 0.0)
        kp_ref[0, rows, :] = (kv2[:, :nq] + jnp.concatenate([kpe] * H_B, axis=1)).astype(BF16)
        vp_ref[0, rows, :] = (kv2[:, nq:] + _value_ones(nq)).astype(BF16)

    w = convw_ref[...]

    def conv_act(rows):
        a, b = rows.start, rows.stop
        y = (w[0:1] * xpad[5 + a:5 + b, :] + w[1:2] * xpad[6 + a:6 + b, :] + w[2:3] * xpad[7 + a:7 + b, :]
             + w[3:4] * xpad[8 + a:8 + b, :])
        act[rows, :] = _silu(y)

    r, c = _iota2((CHUNK, CHUNK))
    tril = c <= r
    strict = c < r
    tril_b = tril.astype(F32).astype(BF16)
    dtb_row, alog_row = rowc_ref[0:1, :], rowc_ref[1:2, :]

    def l2n(v):
        return v * lax.rsqrt(jnp.sum(v * v, axis=-1, keepdims=True) + EPS)

    def delta(chunks, s):
        items = [(ci, h) for ci in chunks for h in range(H_A)]
        rows = {ci: slice(ci * CHUNK, (ci + 1) * CHUNK) for ci in chunks}
        gc_all, gct, beta = {}, {}, {}
        for ci in chunks:
            sm = small_ref[0, rows[ci], :]
            beta[ci] = _sigmoid(sm)
            gc_all[ci] = _cum_left(tril_b, -jnp.exp(alog_row) * _softplus(sm + dtb_row))
            gct[ci] = jnp.transpose(gc_all[ci])
        qs = [l2n(act[rows[ci], h * DK_A:(h + 1) * DK_A]) * DK_A ** -0.5 for ci, h in items]
        ks = [l2n(act[rows[ci], (H_A + h) * DK_A:(H_A + h + 1) * DK_A]) for ci, h in items]
        vs = [act[rows[ci], 2 * H_A * DK_A + h * DV_A:2 * H_A * DK_A + (h + 1) * DV_A] for ci, h in items]
        bcol = [beta[ci][:, GATE_LANE + h:GATE_LANE + h + 1] for ci, h in items]
        g_lane = GATE_LANE + H_A
        gcol = [gc_all[ci][:, g_lane + h:g_lane + h + 1] for ci, h in items]
        dm = [jnp.exp(jnp.where(tril, gcol[n] - gct[ci][g_lane + h:g_lane + h + 1, :], NEG_BIG))
              for n, (ci, h) in enumerate(items)]
        kb = [k * b for k, b in zip(ks, bcol)]
        kbf = [k.astype(BF16) for k in ks]
        ms = [_mm_nt(a, b) * jnp.where(strict, d, 0.0) for a, b, d in zip(kb, kbf, dm)]
        tinv = _unit_lower_inverse(ms, r, c)
        egc = [jnp.exp(g) for g in gcol]
        uw = [_mm(t, jnp.concatenate([v * b, x * e], axis=1)) for t, v, b, x, e in zip(tinv, vs, bcol, kb, egc)]
        attn = [(_mm_nt(q, k) * d).astype(BF16) for q, k, d in zip(qs, kbf, dm)]
        gl = [g[CHUNK - 1:CHUNK, :] for g in gcol]
        wq = [jnp.concatenate([x[:, DV_A:], q * e], axis=0).astype(BF16) for x, q, e in zip(uw, qs, egc)]
        kgt = [jnp.transpose(k * jnp.exp(l - g)).astype(BF16) for k, l, g in zip(ks, gl, gcol)]
        egl = [jnp.exp(l) for l in gl]
        for pos, ci in enumerate(chunks):
            idx = [pos * H_A + h for h in range(H_A)]
            ws = [_mm(wq[n], s[h]) for h, n in enumerate(idx)]
            v_new = [uw[n][:, :DV_A] - ws[h][:CHUNK] for h, n in enumerate(idx)]
            o = [ws[h][CHUNK:] + _mm(attn[n], v_new[h]) for h, n in enumerate(idx)]
            s = [s[h] * egl[n] + _mm(kgt[n], v_new[h]) for h, n in enumerate(idx)]
            for h in range(H_A):
                lanes = slice(h * DV_A, (h + 1) * DV_A)
                gated = _rms(o[h], gn_ref[...]) * _silu(za_scr[rows[ci], lanes])
                o_ref[0, rows[ci], lanes] = gated.astype(BF16)
        return s

    s = [s_scr[h] for h in range(H_A)]
    for rows in half_rows:
        project(rows)
    for rows in half_rows:
        conv_act(rows)
    s = delta(range(nc), s)
    tail = xpad[tc:tc + 8, :]
    cout_ref[0] = tail
    xpad[0:8, :] = tail
    for h in range(H_A):
        s_scr[h] = s[h]
    sout_ref[0] = s_scr[...]


def _even_layer(x, mod, g_pre, wts, tabs, conv_w, conv_buf, s_delta, a_log, dt_bias, g_norm_a, nc):
    B, T, _ = x.shape
    assert nc == 1 or nc % 2 == 0
    tc = nc * CHUNK
    lane = jnp.zeros((LANES,), F32)
    a_lanes = slice(GATE_LANE + H_A, GATE_LANE + 2 * H_A)
    rowc = jnp.stack([lane.at[a_lanes].set(dt_bias), lane.at[a_lanes].set(a_log)])
    cbuf = jnp.pad(conv_buf, ((0, 0), (8 - (CONV_W - 1), 0), (0, 0)))
    full = lambda a: pl.BlockSpec(a.shape, lambda b, i: (0,) * a.ndim)
    tok = lambda n: pl.BlockSpec((1, tc, n), lambda b, i: (b, i, 0))
    tab = pl.BlockSpec((tc, LANES), lambda b, i: (i, 0))
    state = pl.BlockSpec((1, H_A, DK_A, DV_A), lambda b, i: (b, 0, 0, 0))
    tail = pl.BlockSpec((1, 8, QKV_A), lambda b, i: (b, 0, 0))
    outs = [(W_B, F32), (LANES, F32), (KV_LORA, F32), (H_B * LANES, BF16), (H_B * LANES, BF16), (H_B * LANES, BF16),
            (D_ROPE, F32), (W_A, BF16)]
    res = pl.pallas_call(
        functools.partial(_even_layer_kernel, nc),
        grid=(B, T // tc),
        in_specs=[tok(D), pl.BlockSpec((1, 3, D), lambda b, i: (b, 0, 0)), full(g_pre),
                  full(wts["wbig"]), full(wts["wsm"]), full(wts["gq"]), full(wts["wq"]), full(wts["gkv"]),
                  full(wts["wkv"]), tab, tab, tab, tab, full(conv_w), tail, state, full(rowc), full(g_norm_a)],
        out_specs=[tok(n) for n, _ in outs] + [state, tail],
        out_shape=[jax.ShapeDtypeStruct((B, T, n), dt) for n, dt in outs]
        + [jax.ShapeDtypeStruct((B, H_A, DK_A, DV_A), F32), jax.ShapeDtypeStruct((B, 8, QKV_A), F32)],
        scratch_shapes=[pltpu.VMEM((tc + 8, QKV_A), F32), pltpu.VMEM((tc, QKV_A), F32), pltpu.VMEM((tc, W_A), F32),
                        pltpu.VMEM((H_A, DK_A, DV_A), F32)],
        compiler_params=_cparams(2),
        name="even_layer",
    )(x, mod, g_pre, wts["wbig"], wts["wsm"], wts["gq"], wts["wq"], wts["gkv"], wts["wkv"], *tabs, conv_w, cbuf,
      s_delta, rowc, g_norm_a)
    zb, small, ckv, qp, kp, vp, kr_new, o_a, s_new, ctail = res
    return zb, small, ckv, qp, kp, vp, kr_new, o_a, s_new, ctail[:, 8 - (CONV_W - 1):, :]


def _attn_kernel(tq, q_ref, k_ref, v_ref, zb_ref, o_ref, m_scr, acc_scr):
    i = pl.program_id(2)
    heads = (0, 1)
    hl = [slice(hh * LANES, (hh + 1) * LANES) for hh in heads]
    r, c = _iota2((tq, tq))
    mask = (c >> 6) <= (r >> 6)
    qs = [q_ref[0, :, hl[hh]] for hh in heads]
    m_scr[...] = jnp.full((2, tq, LANES), NEG_BIG, F32)
    acc_scr[...] = jnp.zeros((2, tq, LANES), F32)

    def tile(rows, tk, masked):
        s = [_mm_nt(qs[hh], k_ref[0, rows, hl[hh]]) for hh in heads]
        if masked:
            s = [jnp.where(mask, x, NEG_BIG) for x in s]
        m_prev = [m_scr[hh] for hh in heads]
        m_new = [jnp.maximum(m_prev[hh], jnp.max(s[hh], axis=-1, keepdims=True)) for hh in heads]
        alpha = [jnp.exp2(m_prev[hh] - m_new[hh]) for hh in heads]
        p = [jnp.exp2(s[hh] - jnp.concatenate([m_new[hh]] * (tk // LANES), axis=1)) for hh in heads]
        pv = [_mm(p[hh], v_ref[0, rows, hl[hh]]) for hh in heads]
        for hh in heads:
            acc_scr[hh] = alpha[hh] * acc_scr[hh] + pv[hh]
            m_scr[hh] = m_new[hh]

    def body(j, carry):
        tile(pl.ds(pl.multiple_of(j * 2 * tq, tq), 2 * tq), 2 * tq, False)
        return carry
    lax.fori_loop(0, i // 2, body, 0)

    @pl.when(i % 2 == 1)
    def _():
        tile(pl.ds(pl.multiple_of((i - 1) * tq, tq), tq), tq, False)

    tile(pl.ds(pl.multiple_of(i * tq, tq), tq), tq, True)
    _, lane = _iota2((tq, LANES))
    a0, a1 = acc_scr[0], acc_scr[1]
    o0 = a0 / pltpu.roll(a0, DV_B, 1)
    o1 = a1 / pltpu.roll(a1, DV_B, 1)
    o = jnp.where(lane < DV_B, o0, pltpu.roll(o1, DV_B, 1))
    o_ref[0] = (o * _silu(zb_ref[0])).astype(BF16)


def _attention(qp, kp, vp, zb, tq):
    B, T, _ = qp.shape
    pair = lambda rows, im: pl.BlockSpec((1, rows, 2 * LANES), im)
    return pl.pallas_call(
        functools.partial(_attn_kernel, tq),
        grid=(B, H_B // 2, T // tq),
        in_specs=[pair(tq, lambda b, h, i: (b, i, h)), pair(T, lambda b, h, i: (b, 0, h)),
                  pair(T, lambda b, h, i: (b, 0, h)), pl.BlockSpec((1, tq, LANES), lambda b, h, i: (b, i, h))],
        out_specs=pl.BlockSpec((1, tq, LANES), lambda b, h, i: (b, i, h)),
        out_shape=jax.ShapeDtypeStruct((B, T, W_B), BF16),
        scratch_shapes=[pltpu.VMEM((2, tq, LANES), F32)] * 2,
        compiler_params=_cparams(3),
        name="attn",
    )(qp, kp, vp, zb)


def _attn_decode_kernel(q_ref, cpast_ref, krpast_ref, cnew_ref, small_ref, wx_ref, wuv_ref, zb_ref, o_ref):
    t = q_ref.shape[1]
    q = q_ref[0]
    qx = jnp.concatenate([_mm(q[:, h * LANES:(h + 1) * LANES], wx_ref[h]) for h in range(H_B)], axis=0)
    ql, qr = qx[:, :KV_LORA].astype(BF16), qx[:, KV_LORA:].astype(BF16)
    segs = [(cpast_ref[0, 0].astype(BF16), krpast_ref[0, 0].astype(BF16)),
            (cnew_ref[0].astype(BF16), small_ref[0].astype(BF16))]
    s = [_mm_nt(ql, c) + _mm_nt(qr, kr) for c, kr in segs]
    m = jnp.maximum(jnp.max(s[0], axis=-1, keepdims=True), jnp.max(s[1], axis=-1, keepdims=True))
    p = [jnp.exp2(x - m) for x in s]
    l = jnp.sum(p[0], axis=-1, keepdims=True) + jnp.sum(p[1], axis=-1, keepdims=True)
    ol = (_mm(p[0], segs[0][0]) + _mm(p[1], segs[1][0])) / l
    o = _mm(ol[0:t], wuv_ref[0])
    for h in range(1, H_B):
        o = o + _mm(ol[h * t:(h + 1) * t], wuv_ref[h])
    o_ref[0] = (o * _silu(zb_ref[0])).astype(BF16)


def _attn_decode(qp, ckv_cache, krp_cache, e, ckv_new, small_new, wts, zb):
    B, T, _ = qp.shape
    P = ckv_cache.shape[2]
    row = lambda r, n: pl.BlockSpec((1, r, n), lambda b: (b, 0, 0))
    cache = lambda n: pl.BlockSpec((1, 1, P, n), lambda b: (e, b, 0, 0))
    full = lambda a: pl.BlockSpec(a.shape, lambda b: (0,) * a.ndim)
    return pl.pallas_call(
        _attn_decode_kernel,
        grid=(B,),
        in_specs=[row(T, H_B * LANES), cache(KV_LORA), cache(LANES), row(T, KV_LORA), row(T, LANES),
                  full(wts["wx"]), full(wts["wuv"]), row(T, W_B)],
        out_specs=row(T, W_B),
        out_shape=jax.ShapeDtypeStruct((B, T, W_B), BF16),
        compiler_params=_cparams(1),
        name="attn_decode",
    )(qp, ckv_cache, krp_cache, ckv_new, small_new, wts["wx"], wts["wuv"], zb)


def _odd_layer_kernel(layer, nc, x_ref, oa_ref, ob_ref, wa_ref, wb_ref, modp_ref, gpostp_ref,
                      mod_ref, gpre_ref, w_ref, lbl_ref, s0_ref, gn_ref, wout_ref, gpost_ref,
                      xo_ref, sout_ref, qs_ref, lf_ref, iv_ref, z_ref, o_ref, s_scr, *bufs):
    @pl.when(pl.program_id(1) == 0)
    def _():
        s_scr[...] = s0_ref[0]

    mix = _mm(oa_ref[0], wa_ref[...]) + _mm(ob_ref[0], wb_ref[...])
    x = x_ref[0] + modp_ref[0, 2:3, :] * _rms(mix, gpostp_ref[...])
    hb = _prenorm(x, mod_ref, gpre_ref).astype(BF16)
    n = H_C * DK_C
    lg = lbl_ref[...]
    e = jnp.exp(lg - jnp.max(lg, axis=0, keepdims=True))
    p = e / jnp.sum(e, axis=0, keepdims=True)
    lb = jnp.sum(p[1:layer + 1], axis=0, keepdims=True)

    def project(rows, blocks):
        for blk in blocks:
            pr = jnp.dot(hb[rows], w_ref[:, blk * n:(blk + 1) * n], preferred_element_type=F32)
            if blk == 0:
                qs_ref[rows, :] = _silu(pr)
            elif blk == 1:
                lf_ref[rows, :] = jnp.log(lb + (1.0 - lb) * _sigmoid(pr))
            elif blk == 2:
                iv_ref[rows, :] = pr
            else:
                z_ref[rows, :] = pr

    nsub = CHUNK // SUB
    r2, c2 = _iota2((CHUNK, CHUNK))
    cum_l = (c2 <= r2).astype(F32).astype(BF16)
    rm, cm = _iota2((CHUNK, KS_ROWS))
    seg = jnp.zeros_like(cm)
    off = jnp.zeros_like(cm)
    for si in range(1, nsub):
        start = SUB * si * (si + 1) // 2
        seg = seg + (cm >= start).astype(jnp.int32)
        off = off + jnp.where(cm >= start, SUB * si, 0)
    pmask = ((rm >> 4) == seg) & ((cm - off) <= rm) & (cm < KS_USED)

    heads = range(H_C)
    hl = [slice(h * DK_C, (h + 1) * DK_C) for h in heads]

    def prep(ci, buf):
        qt_s, qg_s, kt_s, ks_s, it_s, dec_s = buf
        rows = slice(ci * CHUNK, (ci + 1) * CHUNK)
        lf = lf_ref[rows, :]
        qs = qs_ref[rows, :]
        ivb = iv_ref[rows, :].astype(BF16)
        g = _cum_left(cum_l, lf)
        wloc = jnp.concatenate([g[:SUB]] + [g[j * SUB:(j + 1) * SUB] - g[j * SUB - 1:j * SUB, :]
                                            for j in range(1, CHUNK // SUB)], axis=0)
        k = 1.0 - jnp.exp(lf)
        sub = [slice(j * SUB, (j + 1) * SUB) for j in range(nsub)]
        bound = [g[(j + 1) * SUB - 1:(j + 1) * SUB, :] for j in range(nsub)]
        diag = (k * jnp.exp(jnp.minimum(-wloc, EXP_CLAMP))).astype(BF16)
        base = [k[sub[j]] * jnp.exp(wloc[(j + 1) * SUB - 1:(j + 1) * SUB, :] - wloc[sub[j]]) for j in range(nsub - 1)]
        pieces = []
        for si in range(nsub):
            for j in range(si):
                far = base[j] if j + 1 == si else base[j] * jnp.exp(bound[si - 1] - bound[j])
                pieces.append(far.astype(BF16))
            pieces.append(diag[sub[si]])
        pieces.append(jnp.zeros((KS_ROWS - KS_USED, W_C), BF16))
        ks_s[...] = jnp.concatenate(pieces, axis=0)
        qt_s[...] = (qs * jnp.exp(wloc)).astype(BF16)
        qg_s[...] = (qs * jnp.exp(g)).astype(BF16)
        gl = g[CHUNK - 1:CHUNK, :]
        kt_s[...] = (k * jnp.exp(gl - g)).astype(BF16)
        dec_s[...] = jnp.broadcast_to(jnp.exp(gl), dec_s.shape)
        it_s[...] = jnp.concatenate([ivb[:(si + 1) * SUB] for si in range(nsub)]
                                    + [jnp.zeros((KS_ROWS - KS_USED, W_C), BF16)], axis=0)

    def mat(ci, buf):
        qt_s, qg_s, kt_s, ks_s, it_s, dec_s = buf
        rows = slice(ci * CHUNK, (ci + 1) * CHUNK)
        last = slice(KS_USED - CHUNK, KS_USED)
        st = [s_scr[h] for h in heads]
        sc = [_mm_nt(qt_s[:, hl[h]], ks_s[:, hl[h]]) for h in heads]
        upd = [_mm_tn(it_s[last, hl[h]], kt_s[:, hl[h]]) for h in heads]
        inter = [_mm_nt(qg_s[:, hl[h]], st[h]) for h in heads]
        p = [jnp.where(pmask, x, 0.0).astype(BF16) for x in sc]
        o = [_mm(p[h], it_s[:, hl[h]]) + inter[h] for h in heads]
        for h in heads:
            s_scr[h] = st[h] * dec_s[0:1, hl[h]] + upd[h]
            o_ref[rows, hl[h]] = _rms(o[h], gn_ref[...]) * _silu(z_ref[rows, hl[h]])

    buf_a, buf_b = bufs[:6], bufs[6:]
    tc = nc * CHUNK
    if nc == 1:
        project(slice(0, tc), range(4))
        prep(0, buf_a)
        mat(0, buf_a)
    else:
        npair = nc // 2
        project(slice(0, tc // 2), range(4))
        prep(0, buf_a)
        for i in range(npair):
            if i < npair // 2:
                per = 4 // (npair // 2)
                project(slice(tc // 2, tc), range(i * per, (i + 1) * per))
            c0 = 2 * i
            prep(c0 + 1, buf_b)
            mat(c0, buf_a)
            if c0 + 2 < nc:
                prep(c0 + 2, buf_a)
            mat(c0 + 1, buf_b)
    sout_ref[0] = s_scr[...]
    acc = _mm(o_ref[...], wout_ref[...])
    xo_ref[0] = x + mod_ref[0, 2:3, :] * _rms(acc, gpost_ref[...])


def _odd_layer(x, prev, mod, g_pre, g_post, w_in, w_out, lb_logits, s_hgrn, g_norm_c, layer, nc):
    oa, ob, wa, wb, modp, gpostp = prev
    B, T, _ = x.shape
    assert nc == 1 or nc % 4 == 0
    tc = nc * CHUNK
    operand_bufs = [pltpu.VMEM((CHUNK, W_C), BF16)] * 3 + [pltpu.VMEM((KS_ROWS, W_C), BF16)] * 2 + [
        pltpu.VMEM((8, W_C), F32)]
    tok = pl.BlockSpec((1, tc, D), lambda b, i: (b, i, 0))
    half = pl.BlockSpec((1, tc, W_A), lambda b, i: (b, i, 0))
    modspec = pl.BlockSpec((1, 3, D), lambda b, i: (b, 0, 0))
    st = pl.BlockSpec((1, H_C, DV_C, DK_C), lambda b, i: (b, 0, 0, 0))
    full = lambda a: pl.BlockSpec(a.shape, lambda b, i: (0,) * a.ndim)
    x_new, s_new = pl.pallas_call(
        functools.partial(_odd_layer_kernel, layer, nc),
        grid=(B, T // tc),
        in_specs=[tok, half, half, full(wa), full(wb), modspec, full(gpostp),
                  modspec, full(g_pre), full(w_in), full(lb_logits), st, full(g_norm_c), full(w_out), full(g_post)],
        out_specs=[tok, st],
        out_shape=[jax.ShapeDtypeStruct((B, T, D), F32), jax.ShapeDtypeStruct((B, H_C, DV_C, DK_C), F32)],
        scratch_shapes=[pltpu.VMEM((tc, W_C), F32)] * 5 + [pltpu.VMEM((H_C, DV_C, DK_C), F32)] + operand_bufs * 2,
        compiler_params=_cparams(2),
        name="odd_layer",
    )(x, oa, ob, wa, wb, modp, gpostp, mod, g_pre, w_in, lb_logits, jnp.swapaxes(s_hgrn, -1, -2), g_norm_c, w_out,
      g_post)
    return x_new, jnp.swapaxes(s_new, -1, -2)


def _even_weights(w_in, g_q, w_uq, g_kv, w_ukv):
    o_b, o_a, o_za = QKV_A, QKV_A + H_A, QKV_A + 2 * H_A
    o_cq = o_za + W_A
    o_ckv = o_cq + Q_LORA
    o_kr = o_ckv + KV_LORA
    o_zb = o_kr + D_ROPE
    wbig = jnp.concatenate([w_in[:, :QKV_A], w_in[:, o_za:o_cq], w_in[:, o_cq:o_ckv], w_in[:, o_ckv:o_kr],
                            w_in[:, o_zb:]], axis=1)
    kr = w_in[:, o_kr:o_zb]
    z = lambda n: jnp.zeros((D, n), F32)
    wsm = jnp.concatenate([z(KR_LANE), kr, w_in[:, o_b:o_za], z(LANES - GATE_LANE - 2 * H_A)], axis=1)
    w3 = w_uq.reshape(Q_LORA, H_B, D_NOPE + D_ROPE)
    wq = jnp.concatenate([w3, jnp.zeros((Q_LORA, H_B, LANES - D_NOPE - D_ROPE), F32)], -1)
    wq = wq.reshape(Q_LORA, H_B * LANES)
    k3 = w_ukv.reshape(KV_LORA, H_B, D_NOPE + DV_B)
    wk = jnp.concatenate([k3[..., :D_NOPE], jnp.zeros((KV_LORA, H_B, LANES - D_NOPE), F32)], -1)
    wv = jnp.concatenate([k3[..., D_NOPE:], jnp.zeros((KV_LORA, H_B, LANES - DV_B), F32)], -1)
    wkv = jnp.concatenate([wk.reshape(KV_LORA, H_B * LANES), wv.reshape(KV_LORA, H_B * LANES)], axis=1)
    nx = KV_LORA + LANES
    place = jnp.eye(D_ROPE, nx, k=KV_LORA + KR_LANE, dtype=F32)
    wx = jnp.concatenate([
        jnp.pad(jnp.transpose(k3[..., :D_NOPE], (1, 2, 0)), ((0, 0), (0, 0), (0, nx - KV_LORA))),
        jnp.broadcast_to(place, (H_B, D_ROPE, nx)),
        jnp.zeros((H_B, LANES - D_NOPE - D_ROPE, nx), F32)], axis=1)
    uv = jnp.transpose(k3[..., D_NOPE:], (1, 0, 2))
    wuv = (uv[:, :, None, :] * jnp.eye(H_B, dtype=F32)[:, None, :, None]).reshape(H_B, KV_LORA, W_B)
    return dict(wbig=wbig.astype(BF16), wsm=wsm.astype(BF16), wq=wq.astype(BF16), wkv=wkv.astype(BF16),
                gq=g_q.reshape(1, Q_LORA), gkv=g_kv.reshape(1, KV_LORA),
                wx=wx.astype(BF16), wuv=wuv.astype(BF16))


def _rope_tables(pos):
    half = D_ROPE // 2
    inv = ROPE_THETA ** (-jnp.arange(half, dtype=F32) / half)
    ang = pos.astype(F32)[:, None] * inv[None, :]
    cos, sin = jnp.cos(ang), jnp.sin(ang)
    t = pos.shape[0]
    one, zero = jnp.ones, jnp.zeros
    scale = (D_NOPE + D_ROPE) ** -0.5 * math.log2(math.e)
    pad_q = LANES - D_NOPE - D_ROPE
    pad_k = LANES - KR_LANE - D_ROPE
    cq = scale * jnp.concatenate([one((t, D_NOPE), F32), cos, cos, zero((t, pad_q), F32)], axis=1)
    sq = scale * jnp.concatenate([zero((t, D_NOPE), F32), -sin, sin, zero((t, pad_q), F32)], axis=1)
    ck = jnp.concatenate([one((t, KR_LANE), F32), cos, cos, one((t, pad_k), F32)], axis=1)
    sk = jnp.concatenate([zero((t, KR_LANE), F32), -sin, sin, zero((t, pad_k), F32)], axis=1)
    return cq, sq, ck, sk


def _tiles(T):
    nc = min(8, T // CHUNK)
    tq = min(512, T)
    return nc, tq


def _trunk(x, mod, ckv_past, kr_past, s_delta, s_conv, s_hgrn, prm, tiles=None):
    B, T, _ = x.shape
    nc, tq = tiles or _tiles(T)
    past = 0 if ckv_past is None else ckv_past.shape[2]
    tabs = _rope_tables(past + jnp.arange(T, dtype=jnp.int32))
    if past:
        krp_past = jnp.pad(kr_past, ((0, 0), (0, 0), (0, 0), (KR_LANE, LANES - KR_LANE - D_ROPE)))
    nd, ncv, nk, nr, nh = [], [], [], [], []
    for l in range(DEPTH):
        m = mod[l].reshape(B, 3, D)
        g_pre, g_post = prm["g_pre"][l].reshape(1, D), prm["g_post"][l].reshape(1, D)
        if l % 2 == 0:
            e = l // 2
            wts = prm["even"][e]
            zb, small, ckv, qp, kp, vp, kr_new, o_a, s_new, conv_new = _even_layer(
                x, m, g_pre, wts, tabs, prm["conv_w"][e], s_conv[e], s_delta[e], prm["a_log"][e], prm["dt_bias"][e],
                prm["g_norm_a"][e].reshape(1, DV_A), nc)
            if past:
                assert past % CHUNK == 0 and T <= CHUNK
                o_b = _attn_decode(qp, ckv_past, krp_past, e, ckv, small, wts, zb)
            else:
                o_b = _attention(qp, kp, vp, zb, tq)
            w_out = prm["w_out_e"][e]
            pending = (o_a, o_b, w_out[:W_A], w_out[W_A:], m, g_post)
            nd.append(s_new)
            ncv.append(conv_new)
            nk.append(ckv)
            nr.append(kr_new)
        else:
            j = l // 2
            x, s_new = _odd_layer(x, pending, m, g_pre, g_post, prm["w_in_o"][j], prm["w_out_o"][j], prm["lb_logits"],
                                  s_hgrn[j], prm["g_norm_c"][j].reshape(1, DV_C), l, nc)
            nh.append(s_new)
    return (x, jnp.stack(nd), jnp.stack(ncv), jnp.stack(nk), jnp.stack(nr), jnp.stack(nh))


def _forward(x_prompt, x_sample, c_prompt, c_sample, cache_ckv, cache_kr, state_delta, state_conv, state_hgrn,
             w_ada, b_ada, g_pre, g_post, w_in_e, conv_w, a_log, dt_bias, g_norm_a, g_q, w_uq, g_kv, w_ukv,
             w_out_e, w_in_o, lb_logits, g_norm_c, w_out_o, tiles_prompt=None, tiles_sample=None):
    bp, bs = x_prompt.shape[0], x_sample.shape[0]
    n_even, n_odd = w_in_e.shape[0], w_in_o.shape[0]
    mod = _ada(jnp.concatenate([c_prompt, c_sample], axis=0), w_ada, b_ada)
    prm = dict(
        g_pre=g_pre, g_post=g_post, conv_w=conv_w, a_log=a_log, dt_bias=dt_bias, g_norm_a=g_norm_a,
        g_norm_c=g_norm_c, lb_logits=lb_logits,
        even=[_even_weights(w_in_e[e], g_q[e], w_uq[e], g_kv[e], w_ukv[e]) for e in range(n_even)],
        w_out_e=w_out_e.astype(BF16), w_in_o=w_in_o.astype(BF16), w_out_o=w_out_o.astype(BF16))
    zeros = lambda *s: jnp.zeros(s, F32)
    outs_p = _trunk(x_prompt, mod[:, :bp], None, None, zeros(n_even, bp, H_A, DK_A, DV_A),
                    zeros(n_even, bp, CONV_W - 1, QKV_A), zeros(n_odd, bp, H_C, DK_C, DV_C), prm, tiles_prompt)
    outs_s = _trunk(x_sample, mod[:, bp:], cache_ckv, cache_kr, state_delta, state_conv, state_hgrn, prm,
                    tiles_sample)
    return (outs_p[0], outs_s[0]) + outs_p[1:] + outs_s[1:]


def kernel(x_prompt, x_sample, c_prompt, c_sample, cache_ckv, cache_kr, state_delta, state_conv, state_hgrn,
           w_ada, b_ada, g_pre, g_post, w_in_e, conv_w, a_log, dt_bias, g_norm_a, g_q, w_uq, g_kv, w_ukv,
           w_out_e, w_in_o, lb_logits, g_norm_c, w_out_o):
    return _forward(x_prompt, x_sample, c_prompt, c_sample, cache_ckv, cache_kr, state_delta, state_conv,
                    state_hgrn, w_ada, b_ada, g_pre, g_post, w_in_e, conv_w, a_log, dt_bias, g_norm_a, g_q, w_uq,
                    g_kv, w_ukv, w_out_e, w_in_o, lb_logits, g_norm_c, w_out_o)
```
